```python
import math
import jax, jax.numpy as jnp
from jax import lax
import numpy as np

D_MODEL = 2048
BATCH = 4
SEQ = 4096
DEPTH = 4

GRID_W = 64
CTX_LEN = 256
N_MIXERS = 2
N_DN_LAYERS = (DEPTH + 1) // 2
N_CV_LAYERS = DEPTH // 2
DN_QK_HEADS = 16
DN_V_HEADS = 32
DN_HEAD_K = 128
DN_HEAD_V = 128
DN_QK_DIM = DN_QK_HEADS * DN_HEAD_K
DN_V_DIM = DN_V_HEADS * DN_HEAD_V
DN_CONV_DIM = 2 * DN_QK_DIM + DN_V_DIM
DN_N_DIR = 2
DN_IN_DIM = DN_CONV_DIM + DN_V_DIM + DN_N_DIR * 2 * DN_V_HEADS
DN_CHUNK = 64
SHORT_CONV = 5
CV_INNER = D_MODEL
CV_KERNEL = 31
N_GROUPS = 8
EXPERTS_PER_GROUP = 8
N_EXPERTS = N_GROUPS * EXPERTS_PER_GROUP
TOP_K_IN_GROUP = 2
D_EXPERT = 384
MOE_BLOCK = 128
EPS = 1e-6

kernel_name = "hybrid_deltanet_conformer_hmoe_dit"


def rmsnorm(x, g):
    xf = x.astype(jnp.float32)
    y = xf * lax.rsqrt(jnp.mean(xf * xf, axis=-1, keepdims=True) + EPS)
    return y.astype(x.dtype) * g


def layernorm(x, g, b):
    xf = x.astype(jnp.float32)
    mu = jnp.mean(xf, axis=-1, keepdims=True)
    var = jnp.mean(jnp.square(xf - mu), axis=-1, keepdims=True)
    return ((xf - mu) * lax.rsqrt(var + EPS)).astype(x.dtype) * g + b


def l2norm(x):
    return x * lax.rsqrt(jnp.sum(x * x, axis=-1, keepdims=True) + EPS)


def dwconv_centred(x, w):
    k, ch = w.shape
    return lax.conv_general_dilated(
        x, w[:, None, :], window_strides=(1,), padding=[(k // 2, k // 2)],
        dimension_numbers=("NWC", "WIO", "NWC"), feature_group_count=ch)


def to_column_major(h):
    b, n, d = h.shape
    rows = n // GRID_W
    return h.reshape(b, rows, GRID_W, d).transpose(0, 2, 1, 3).reshape(b, n, d)


def from_column_major(h):
    b, n, d = h.shape
    rows = n // GRID_W
    return h.reshape(b, GRID_W, rows, d).transpose(0, 2, 1, 3).reshape(b, n, d)


def chunk_gated_delta(q, k, v, g, beta, s0):
    b_, h_, n_tok, dk = q.shape
    dv = v.shape[-1]
    cs = DN_CHUNK
    nc = n_tok // cs
    q = (q * dk ** -0.5).reshape(b_, h_, nc, cs, dk)
    k = k.reshape(b_, h_, nc, cs, dk)
    v = v.reshape(b_, h_, nc, cs, dv)
    beta = beta.reshape(b_, h_, nc, cs, 1)
    gc = jnp.cumsum(g.reshape(b_, h_, nc, cs), axis=-1)
    incl = jnp.tril(jnp.ones((cs, cs), bool))
    strict = jnp.tril(jnp.ones((cs, cs), bool), -1)
    decay = jnp.exp(jnp.where(incl, gc[..., :, None] - gc[..., None, :], -jnp.inf))
    kb = k * beta
    m = jnp.where(strict, jnp.einsum("bhncd,bhnsd->bhncs", kb, k) * decay, 0.0)
    a_mat = m + jnp.eye(cs, dtype=m.dtype)
    rhs = jnp.concatenate([v * beta, kb * jnp.exp(gc)[..., None]], axis=-1)
    sol = lax.linalg.triangular_solve(a_mat, rhs, left_side=True, lower=True)
    u, w = sol[..., :dv], sol[..., dv:]
    qk = jnp.einsum("bhncd,bhnsd->bhncs", q, k) * decay
    q_dec = q * jnp.exp(gc)[..., None]
    k_dec = k * jnp.exp(gc[..., -1:] - gc)[..., None]
    g_last = jnp.exp(gc[..., -1])

    def step(s, inp):
        qk_i, q_dec_i, k_dec_i, u_i, w_i, gl_i = inp
        v_new = u_i - jnp.einsum("bhcd,bhde->bhce", w_i, s)
        o_i = jnp.einsum("bhcd,bhde->bhce", q_dec_i, s) + jnp.einsum("bhcs,bhse->bhce", qk_i, v_new)
        s = s * gl_i[..., None, None] + jnp.einsum("bhcd,bhce->bhde", k_dec_i, v_new)
        return s, o_i

    xs = tuple(jnp.moveaxis(t, 2, 0) for t in (qk, q_dec, k_dec, u, w, g_last))
    s_fin, o = lax.scan(step, s0, xs)
    o = jnp.moveaxis(o, 0, 2).reshape(b_, h_, n_tok, dv)
    return o, s_fin


def gated_deltanet(hc, hl, w_in, conv_w, a_log, dt_bias, onorm_g, w_out):
    bsz, lc, _ = hc.shape
    n_tok = lc + hl.shape[1]
    proj = jnp.einsum("btd,de->bte", jnp.concatenate([hc, hl], axis=1), w_in)
    qkv = proj[..., :DN_CONV_DIM]
    z = proj[..., DN_CONV_DIM:DN_CONV_DIM + DN_V_DIM]
    ab = proj[..., DN_CONV_DIM + DN_V_DIM:].astype(jnp.float32).reshape(bsz, n_tok, DN_N_DIR, 2, DN_V_HEADS)
    qkv = jax.nn.silu(jnp.concatenate(
        [dwconv_centred(qkv[:, :lc], conv_w), dwconv_centred(qkv[:, lc:], conv_w)], axis=1)).astype(jnp.float32)
    rep = DN_V_HEADS // DN_QK_HEADS
    q = l2norm(qkv[..., :DN_QK_DIM].reshape(bsz, n_tok, DN_QK_HEADS, DN_HEAD_K))
    k = l2norm(qkv[..., DN_QK_DIM:2 * DN_QK_DIM].reshape(bsz, n_tok, DN_QK_HEADS, DN_HEAD_K))
    q = jnp.repeat(q, rep, axis=2).transpose(0, 2, 1, 3)
    k = jnp.repeat(k, rep, axis=2).transpose(0, 2, 1, 3)
    v = qkv[..., 2 * DN_QK_DIM:].reshape(bsz, n_tok, DN_V_HEADS, DN_HEAD_V).transpose(0, 2, 1, 3)
    g = -jnp.exp(a_log.astype(jnp.float32)) * jax.nn.softplus(ab[:, :, :, 0, :] + dt_bias.astype(jnp.float32))
    beta = jax.nn.sigmoid(ab[:, :, :, 1, :])
    g = jnp.transpose(g, (0, 2, 3, 1))
    beta = jnp.transpose(beta, (0, 2, 3, 1))
    s0 = jnp.zeros((bsz, DN_V_HEADS, DN_HEAD_K, DN_HEAD_V), jnp.float32)
    rev = lambda t: jnp.flip(t, axis=2)
    qc, ql = q[:, :, :lc], q[:, :, lc:]
    kc, kl = k[:, :, :lc], k[:, :, lc:]
    vc, vl = v[:, :, :lc], v[:, :, lc:]
    gf, bf = g[:, 0], beta[:, 0]
    oc_f, sc_f = chunk_gated_delta(qc, kc, vc, gf[..., :lc], bf[..., :lc], s0)
    ol_f, _ = chunk_gated_delta(ql, kl, vl, gf[..., lc:], bf[..., lc:], sc_f)
    gb, bb = g[:, 1], beta[:, 1]
    oc_b, sc_b = chunk_gated_delta(rev(qc), rev(kc), rev(vc), rev(gb[..., :lc]), rev(bb[..., :lc]), s0)
    ol_b, _ = chunk_gated_delta(rev(ql), rev(kl), rev(vl), rev(gb[..., lc:]), rev(bb[..., lc:]), sc_b)
    o = jnp.concatenate([oc_f + rev(oc_b), ol_f + rev(ol_b)], axis=2)
    o = jnp.transpose(o, (0, 2, 1, 3))
    o = rmsnorm(o, onorm_g.astype(jnp.float32)) * jax.nn.silu(
        z.astype(jnp.float32).reshape(bsz, n_tok, DN_V_HEADS, DN_HEAD_V))
    y = jnp.einsum("bte,ed->btd", o.reshape(bsz, n_tok, DN_V_DIM).astype(hc.dtype), w_out)
    return y[:, :lc], y[:, lc:]


def conformer_conv(h, w1, b1, dw, dwb, ln_g, ln_b, w2, b2):
    u = jnp.einsum("btd,de->bte", h, w1) + b1
    a, gate = jnp.split(u, 2, axis=-1)
    u = a * jax.nn.sigmoid(gate)
    u = dwconv_centred(u, dw) + dwb
    u = jax.nn.silu(layernorm(u, ln_g, ln_b))
    return jnp.einsum("bte,ed->btd", u, w2) + b2


def expert_dispatch(h, expert_ids, weights, w_gu, w_down):
    n_t, d = h.shape
    n_assign = n_t * TOP_K_IN_GROUP
    e_flat = expert_ids.reshape(-1)
    tok_flat = jnp.repeat(jnp.arange(n_t, dtype=jnp.int32), TOP_K_IN_GROUP)
    order = jnp.argsort(e_flat)
    e_sorted = e_flat[order]
    counts = jnp.zeros((N_EXPERTS,), jnp.int32).at[e_flat].add(1)
    padded = (counts + MOE_BLOCK - 1) // MOE_BLOCK * MOE_BLOCK
    starts = jnp.cumsum(counts) - counts
    pad_ends = jnp.cumsum(padded)
    pad_starts = pad_ends - padded
    slot = pad_starts[e_sorted] + jnp.arange(n_assign, dtype=jnp.int32) - starts[e_sorted]
    n_blocks = -(-n_assign // MOE_BLOCK) + N_EXPERTS
    n_slots = n_blocks * MOE_BLOCK
    slot_tok = jnp.full((n_slots,), n_t, jnp.int32).at[slot].set(tok_flat[order])
    slot_w = jnp.zeros((n_slots,), h.dtype).at[slot].set(weights.reshape(-1)[order])
    block_expert = jnp.minimum(
        jnp.searchsorted(pad_ends, jnp.arange(n_blocks, dtype=jnp.int32) * MOE_BLOCK, side="right"),
        N_EXPERTS - 1)
    h_pad = jnp.concatenate([h, jnp.zeros((1, d), h.dtype)], axis=0)
    xb = h_pad[slot_tok].reshape(n_blocks, MOE_BLOCK, d)

    def expert_block(args):
        xblk, e = args
        gate, up = jnp.split(xblk @ w_gu[e], 2, axis=-1)
        return (jax.nn.silu(gate) * up) @ w_down[e]

    yb = lax.map(expert_block, (xb, block_expert)).reshape(n_slots, d)
    y = jnp.zeros((n_t + 1, d), h.dtype).at[slot_tok].add(yb * slot_w[:, None])
    return y[:n_t]


def hier_moe(h, w_grp, b_grp, w_exp, b_exp, w_gu, w_down):
    n_t = h.shape[0]
    grp_prob = jax.nn.softmax((h @ w_grp).astype(jnp.float32) + b_grp.astype(jnp.float32), axis=-1)
    grp_p, grp_idx = lax.top_k(grp_prob, 1)
    exp_logits = ((h @ w_exp).astype(jnp.float32) + b_exp.astype(jnp.float32)).reshape(
        n_t, N_GROUPS, EXPERTS_PER_GROUP)
    in_grp = exp_logits[jnp.arange(n_t), grp_idx[:, 0]]
    top_p, top_i = lax.top_k(jax.nn.softmax(in_grp, axis=-1), TOP_K_IN_GROUP)
    weights = grp_p * top_p / jnp.sum(top_p, axis=-1, keepdims=True)
    expert_ids = grp_idx * EXPERTS_PER_GROUP + top_i
    return expert_dispatch(h, expert_ids, weights.astype(h.dtype), w_gu, w_down)


def setup_inputs(seed: int = 0) -> dict:
    key = jax.random.key(seed)
    ks = jax.random.split(key, 40)
    D = D_MODEL
    f32 = jnp.float32
    nrm = lambda k, shape, s: jax.random.normal(k, shape, f32) * s
    dt = jnp.exp(jax.random.uniform(ks[11], (N_DN_LAYERS, DN_N_DIR, DN_V_HEADS), f32,
                                    math.log(1e-3), math.log(1e-1)))
    return {
        "x": nrm(ks[0], (BATCH, SEQ, D), 1.0),
        "c": nrm(ks[1], (BATCH, D), 1.0),
        "ctx": nrm(ks[2], (BATCH, CTX_LEN, D), 1.0),
        "c_ctx": nrm(ks[3], (D,), 1.0),
        "ada_w": nrm(ks[4], (DEPTH, D, 6 * D), 0.5 * D ** -0.5),
        "ada_b": nrm(ks[5], (DEPTH, 6 * D), 0.02),
        "norm1_g": 1.0 + nrm(ks[6], (DEPTH, D), 0.05),
        "norm2_g": 1.0 + nrm(ks[7], (DEPTH, D), 0.05),
        "dn_w_in": nrm(ks[8], (N_DN_LAYERS, D, DN_IN_DIM), D ** -0.5),
        "dn_conv_w": nrm(ks[9], (N_DN_LAYERS, SHORT_CONV, DN_CONV_DIM), SHORT_CONV ** -0.5),
        "dn_a_log": jnp.log(jax.random.uniform(ks[10], (N_DN_LAYERS, DN_N_DIR, DN_V_HEADS), f32, 1.0, 16.0)),
        "dn_dt_bias": dt + jnp.log(-jnp.expm1(-dt)),
        "dn_onorm_g": 1.0 + nrm(ks[12], (N_DN_LAYERS, DN_HEAD_V), 0.05),
        "dn_w_out": nrm(ks[13], (N_DN_LAYERS, DN_V_DIM, D), DN_V_DIM ** -0.5),
        "cv_w1": nrm(ks[14], (N_CV_LAYERS, D, 2 * CV_INNER), D ** -0.5),
        "cv_b1": nrm(ks[15], (N_CV_LAYERS, 2 * CV_INNER), 0.02),
        "cv_dw": nrm(ks[16], (N_CV_LAYERS, CV_KERNEL, CV_INNER), CV_KERNEL ** -0.5),
        "cv_dwb": nrm(ks[17], (N_CV_LAYERS, CV_INNER), 0.02),
        "cv_ln_g": 1.0 + nrm(ks[18], (N_CV_LAYERS, CV_INNER), 0.05),
        "cv_ln_b": nrm(ks[19], (N_CV_LAYERS, CV_INNER), 0.02),
        "cv_w2": nrm(ks[20], (N_CV_LAYERS, CV_INNER, D), CV_INNER ** -0.5),
        "cv_b2": nrm(ks[21], (N_CV_LAYERS, D), 0.02),
        "moe_w_grp": nrm(ks[22], (DEPTH, D, N_GROUPS), D ** -0.5),
        "moe_b_grp": nrm(ks[23], (DEPTH, N_GROUPS), 0.01),
        "moe_w_exp": nrm(ks[24], (DEPTH, D, N_EXPERTS), D ** -0.5),
        "moe_b_exp": nrm(ks[25], (DEPTH, N_EXPERTS), 0.01),
        "moe_w_gu": nrm(ks[26], (DEPTH, N_EXPERTS, D, 2 * D_EXPERT), D ** -0.5),
        "moe_w_down": nrm(ks[27], (DEPTH, N_EXPERTS, D_EXPERT, D), D_EXPERT ** -0.5),
        "final_g": 1.0 + nrm(ks[28], (D,), 0.05),
    }


def reference(x, c, ctx, c_ctx, ada_w, ada_b, norm1_g, norm2_g, dn_w_in, dn_conv_w, dn_a_log,
              dn_dt_bias, dn_onorm_g, dn_w_out, cv_w1, cv_b1, cv_dw, cv_dwb, cv_ln_g, cv_ln_b,
              cv_w2, cv_b2, moe_w_grp, moe_b_grp, moe_w_exp, moe_b_exp, moe_w_gu, moe_w_down, final_g):
    bsz, n_lat, d = x.shape
    lc = ctx.shape[1]
    xl, xc = x, ctx
    for i in range(DEPTH):
        last = i == DEPTH - 1
        j = i // N_MIXERS
        mod_l = (jax.nn.silu(c) @ ada_w[i] + ada_b[i])[:, None, :]
        mod_c = jax.nn.silu(c_ctx) @ ada_w[i] + ada_b[i]
        sh1, sc1, gt1, sh2, sc2, gt2 = jnp.split(mod_l, 6, axis=-1)
        csh1, csc1, cgt1, csh2, csc2, cgt2 = jnp.split(mod_c, 6, axis=-1)
        hl = rmsnorm(xl, norm1_g[i]) * (1.0 + sc1) + sh1
        hc = rmsnorm(xc, norm1_g[i]) * (1.0 + csc1) + csh1
        col_major = (i // N_MIXERS) % 2 == 1
        if col_major:
            hl = to_column_major(hl)
        if i % N_MIXERS == 0:
            yc, yl = gated_deltanet(hc, hl, dn_w_in[j], dn_conv_w[j], dn_a_log[j], dn_dt_bias[j],
                                    dn_onorm_g[j], dn_w_out[j])
        else:
            cv = (cv_w1[j], cv_b1[j], cv_dw[j], cv_dwb[j], cv_ln_g[j], cv_ln_b[j], cv_w2[j], cv_b2[j])
            yl = conformer_conv(hl, *cv)
            yc = None if last else conformer_conv(hc, *cv)
        if col_major:
            yl = from_column_major(yl)
        xl = xl + gt1 * yl
        moe = (moe_w_grp[i], moe_b_grp[i], moe_w_exp[i], moe_b_exp[i], moe_w_gu[i], moe_w_down[i])
        hl2 = rmsnorm(xl, norm2_g[i]) * (1.0 + sc2) + sh2
        if last:
            xl = xl + gt2 * hier_moe(hl2.reshape(bsz * n_lat, d), *moe).reshape(bsz, n_lat, d)
        else:
            xc = xc + cgt1 * yc
            hc2 = rmsnorm(xc, norm2_g[i]) * (1.0 + csc2) + csh2
            y = hier_moe(jnp.concatenate([hc2.reshape(bsz * lc, d), hl2.reshape(bsz * n_lat, d)], axis=0), *moe)
            xc = xc + cgt2 * y[:bsz * lc].reshape(bsz, lc, d)
            xl = xl + gt2 * y[bsz * lc:].reshape(bsz, n_lat, d)
    return rmsnorm(xl, final_g)
```

```python
import functools
import math

import jax
import jax.numpy as jnp
from jax import lax
from jax.experimental import pallas as pl
from jax.experimental.pallas import tpu as pltpu

F32 = jnp.float32
BF16 = jnp.bfloat16
EPS = 1e-6

GRID_W = 64
DN_HEAD = 128
DN_CHUNK = 256
INV_BASE = 16
N_GROUPS = 8
EXPERTS_PER_GROUP = 8
N_EXPERTS = N_GROUPS * EXPERTS_PER_GROUP
MOE_BLOCK = 128
ROUTER_LANES = 128
CONV_HALO = 16
V7X_VMEM_LIMIT = 56 * 1024 * 1024


def _cparams(sem, vmem=V7X_VMEM_LIMIT):
    return pltpu.CompilerParams(dimension_semantics=sem, vmem_limit_bytes=vmem)


def _dot(a, b):
    return jnp.dot(a, b, preferred_element_type=F32)


def _dot_nt(a, b):
    return lax.dot_general(a, b, (((1,), (1,)), ((), ())), preferred_element_type=F32)


def _split(x):
    hi = x.astype(BF16)
    lo = (x - hi.astype(F32)).astype(BF16)
    return hi, lo


def _dot3(a, b):
    ah, al = _split(a)
    bh, bl = _split(b)
    return _dot(ah, bh) + _dot(al, bh) + _dot(ah, bl)


def _silu(x):
    return x * jax.nn.sigmoid(x)


def _norm_mod(x, g, sc, sh):
    ms = jnp.mean(x * x, axis=-1, keepdims=True)
    return (x * lax.rsqrt(ms + EPS)) * g * (1.0 + sc) + sh


def _ada_kernel(c_ref, w_ref, b_ref, o_ref):
    s = _silu(c_ref[...])
    o_ref[...] = _dot3(s, w_ref[...]) + b_ref[...]


def _ada_mod(cvec, ada_w, ada_b):
    depth, d, n = ada_w.shape
    tn = 1024
    return pl.pallas_call(
        _ada_kernel,
        out_shape=jax.ShapeDtypeStruct((depth, 8, n), F32),
        grid=(depth, n // tn),
        in_specs=[
            pl.BlockSpec((8, d), lambda l, j: (0, 0)),
            pl.BlockSpec((None, d, tn), lambda l, j: (l, 0, j)),
            pl.BlockSpec((None, 1, tn), lambda l, j: (l, 0, j)),
        ],
        out_specs=pl.BlockSpec((None, 8, tn), lambda l, j: (l, 0, j)),
        compiler_params=_cparams(("parallel", "parallel")),
        name="ada_mod",
    )(cvec, ada_w, ada_b.reshape(depth, 1, n))


def _mod_spec(d, chunk, row_fn):
    return pl.BlockSpec((None, 1, d), lambda i, j: (row_fn(i), 0, chunk))


def _dn_in_kernel(x_ref, sc_ref, sh_ref, g_ref, w_ref, wab_ref, o_ref, ab_ref, h_scr):
    @pl.when(pl.program_id(1) == 0)
    def _():
        h = _norm_mod(x_ref[...], g_ref[...], sc_ref[...], sh_ref[...])
        h_scr[...] = h.astype(BF16)
        ab_ref[...] = _dot3(h, wab_ref[...])

    o_ref[...] = _dot(h_scr[...], w_ref[...]).astype(o_ref.dtype)


def _dn_in_proj(x2, mod3, g, w_bf, wab, row_fn, tm):
    m, d = x2.shape
    n = w_bf.shape[1]
    nab = wab.shape[1]
    tn = 1024
    return pl.pallas_call(
        _dn_in_kernel,
        out_shape=(jax.ShapeDtypeStruct((m, n), BF16), jax.ShapeDtypeStruct((m, nab), F32)),
        grid=(m // tm, n // tn),
        in_specs=[
            pl.BlockSpec((tm, d), lambda i, j: (i, 0)),
            _mod_spec(d, 1, row_fn),
            _mod_spec(d, 0, row_fn),
            pl.BlockSpec((1, d), lambda i, j: (0, 0)),
            pl.BlockSpec((d, tn), lambda i, j: (0, j)),
            pl.BlockSpec((d, nab), lambda i, j: (0, 0)),
        ],
        out_specs=(pl.BlockSpec((tm, tn), lambda i, j: (i, j)),
                   pl.BlockSpec((tm, nab), lambda i, j: (i, 0))),
        scratch_shapes=[pltpu.VMEM((tm, d), BF16)],
        compiler_params=_cparams(("parallel", "arbitrary")),
        name="dn_in_proj",
    )(x2, mod3, mod3, g.reshape(1, d), w_bf, wab)


def _dn_conv_kernel(cur_ref, prev_ref, next_ref, w_ref, o_ref, ext_scr, *, tl, taps, l2, n_q_blocks):
    i = pl.program_id(1)
    n_i = pl.num_programs(1)
    c = pl.program_id(2)
    hl = CONV_HALO
    prev = prev_ref[0].astype(F32)
    nxt = next_ref[0].astype(F32)
    ext_scr[0:hl, :] = jnp.where(i > 0, prev, 0.0)
    ext_scr[hl:hl + tl, :] = cur_ref[0].astype(F32)
    ext_scr[hl + tl:, :] = jnp.where(i < n_i - 1, nxt, 0.0)
    acc = None
    for k in range(taps):
        term = w_ref[k:k + 1, :] * ext_scr[pl.ds(hl - taps // 2 + k, tl), :]
        acc = term if acc is None else acc + term
    y = _silu(acc)
    cw = y.shape[1]
    if l2:
        scale = jnp.where(c < n_q_blocks, DN_HEAD ** -0.5, 1.0).astype(F32)
        for hh in range(cw // DN_HEAD):
            ys = y[:, hh * DN_HEAD:(hh + 1) * DN_HEAD]
            ss = jnp.sum(ys * ys, axis=-1, keepdims=True)
            o_ref[0, :, hh * DN_HEAD:(hh + 1) * DN_HEAD] = (
                ys * (lax.rsqrt(ss + EPS) * scale)).astype(o_ref.dtype)
    else:
        o_ref[0] = y.astype(o_ref.dtype)


def _dn_conv(proj, conv_w, ch_off, n_ch, l2, qk_dim):
    b, l, _ = proj.shape
    taps = conv_w.shape[0]
    tl = min(l, 512)
    cw = 512
    hl = CONV_HALO
    coff = ch_off // cw
    kern = functools.partial(_dn_conv_kernel, tl=tl, taps=taps, l2=l2, n_q_blocks=qk_dim // cw)
    rpb = tl // hl
    last = l // hl - 1
    return pl.pallas_call(
        kern,
        out_shape=jax.ShapeDtypeStruct((b, l, n_ch), BF16),
        grid=(b, l // tl, n_ch // cw),
        in_specs=[
            pl.BlockSpec((1, tl, cw), lambda bi, i, c: (bi, i, c + coff)),
            pl.BlockSpec((1, hl, cw), lambda bi, i, c: (bi, jnp.maximum(i * rpb - 1, 0), c + coff)),
            pl.BlockSpec((1, hl, cw), lambda bi, i, c: (bi, jnp.minimum((i + 1) * rpb, last), c + coff)),
            pl.BlockSpec((taps, cw), lambda bi, i, c: (0, c + coff)),
        ],
        out_specs=pl.BlockSpec((1, tl, cw), lambda bi, i, c: (bi, i, c)),
        scratch_shapes=[pltpu.VMEM((tl + 2 * hl, cw), F32)],
        compiler_params=_cparams(("parallel", "parallel", "parallel")),
        name="dn_conv_qk" if l2 else "dn_conv_v",
    )(proj, proj, proj, conv_w)


def _dn_gate_kernel(ab_ref, a_ref, dt_ref, isa_ref, isf_ref, o_ref):
    x = ab_ref[0]
    c = x.shape[0]
    z = x + dt_ref[...]
    softplus = jnp.maximum(z, 0.0) + jnp.log(1.0 + jnp.exp(-jnp.abs(z)))
    g = -a_ref[...] * softplus
    beta = jax.nn.sigmoid(x)
    ri = lax.broadcasted_iota(jnp.int32, (c, c), 0)
    ci = lax.broadcasted_iota(jnp.int32, (c, c), 1)
    tri_f = jnp.where(ri >= ci, 1.0, 0.0).astype(BF16)
    tri_b = jnp.where(ri <= ci, 1.0, 0.0).astype(BF16)
    g1 = g.astype(BF16)
    r1 = g - g1.astype(F32)
    g2 = r1.astype(BF16)
    g3 = (r1 - g2.astype(F32)).astype(BF16)
    cum_f = _dot(tri_f, g1) + _dot(tri_f, g2) + _dot(tri_f, g3)
    cum_b = _dot(tri_b, g1) + _dot(tri_b, g2) + _dot(tri_b, g3)
    gc = jnp.where(isf_ref[...] > 0.5, cum_f, cum_b)
    o_ref[0] = jnp.where(isa_ref[...] > 0.5, gc, beta)


def _dn_gates(ab, a_lane, dt_lane, isa_lane, isf_lane):
    b, l, n = ab.shape
    c = DN_CHUNK
    vec = pl.BlockSpec((1, n), lambda bi, i: (0, 0))
    return pl.pallas_call(
        _dn_gate_kernel,
        out_shape=jax.ShapeDtypeStruct((b, l, n), F32),
        grid=(b, l // c),
        in_specs=[pl.BlockSpec((1, c, n), lambda bi, i: (bi, i, 0)), vec, vec, vec, vec],
        out_specs=pl.BlockSpec((1, c, n), lambda bi, i: (bi, i, 0)),
        compiler_params=_cparams(("parallel", "parallel")),
        name="dn_gates",
    )(ab, a_lane, dt_lane, isa_lane, isf_lane)


def _mm1(a, b):
    return _dot(a.astype(BF16), b.astype(BF16))


def _tri_inverse(m, strict_in_block, mm):
    c = m.shape[0]
    ri = lax.broadcasted_iota(jnp.int32, (c, c), 0)
    ci = lax.broadcasted_iota(jnp.int32, (c, c), 1)
    eye = jnp.where(ri == ci, 1.0, 0.0).astype(F32)
    md = jnp.where(strict_in_block, m, 0.0)
    rest = m - md
    t = eye - md
    p = mm(md, md)
    s = 2
    while s < INV_BASE:
        t = t + mm(p, t)
        s *= 2
        if s < INV_BASE:
            p = mm(p, p)
    n = mm(t, rest)
    r = eye - n
    p = mm(n, n)
    s = 2
    while s < c // INV_BASE:
        r = r + mm(p, r)
        s *= 2
        if s < c // INV_BASE:
            p = mm(p, p)
    return mm(r, t)


def _delta_chunk(q, k, v2, gcol, grow_ref, grow_off, lane_g0, s_ref, rev):
    c = q.shape[0]
    ri = lax.broadcasted_iota(jnp.int32, (c, c), 0)
    ci = lax.broadcasted_iota(jnp.int32, (c, c), 1)
    if rev:
        strict, incl = ri < ci, ri <= ci
    else:
        strict, incl = ri > ci, ri >= ci
    sh = jnp.int32(int(math.log2(INV_BASE)))
    same_blk = lax.shift_right_logical(ri, sh) == lax.shift_right_logical(ci, sh)
    strict_in_block = jnp.logical_and(strict, same_blk)
    lane = lax.broadcasted_iota(jnp.int32, gcol.shape, 1)
    kk = _dot_nt(k, k)
    qk = _dot_nt(q, k)
    last = 0 if rev else c - 1
    outs = []
    for s in range(2):
        lg = lane_g0 + s
        lb = lg + gcol.shape[1] // 4
        gc_col = jnp.sum(jnp.where(lane == lg, gcol, 0.0), axis=-1, keepdims=True)
        b_col = jnp.sum(jnp.where(lane == lb, gcol, 0.0), axis=-1, keepdims=True)
        gc_row = grow_ref[0, pl.ds(lg, 1), pl.ds(grow_off, c)]
        decay = jnp.where(incl, jnp.exp(jnp.minimum(gc_col - gc_row, 0.0)), 0.0)
        m = jnp.where(strict, b_col * kk * decay, 0.0)
        ainv = _tri_inverse(m, strict_in_block, _mm1)
        p = qk * decay
        st = s_ref[s]
        sb = st.astype(BF16)
        ks = _dot(k, sb)
        qs = _dot(q, sb)
        eg = jnp.exp(gc_col)
        v = v2[:, s * DN_HEAD:(s + 1) * DN_HEAD].astype(F32)
        rhs = (b_col * (v - eg * ks)).astype(BF16)
        ah, al = _split(ainv)
        v_new = _dot(ah, rhs) + _dot(al, rhs)
        vb = v_new.astype(BF16)
        outs.append(eg * qs + _dot(p.astype(BF16), vb))
        gl = gc_col[last:last + 1, :]
        kd = jnp.exp(gl - gc_col) * k.astype(F32)
        s_ref[s] = jnp.exp(gl) * st + _dot(kd.T.astype(BF16), vb)
    return jnp.concatenate(outs, axis=1)


def _delta_kernel(qc_ref, kc_ref, vc_ref, gcc_ref, grc_ref, ql_ref, kl_ref, vl_ref, gcl_ref, grl_ref,
                  oc_ref, ol_ref, s_ref, *, rev, n_vh):
    h = pl.program_id(1)
    c = DN_CHUNK
    lane_g0 = (2 * n_vh if rev else 0) + 2 * h
    s_ref[...] = jnp.zeros_like(s_ref)
    oc_ref[0] = _delta_chunk(qc_ref[0], kc_ref[0], vc_ref[0], gcc_ref[0], grc_ref, 0, lane_g0,
                             s_ref, rev).astype(oc_ref.dtype)
    n_lat = ql_ref.shape[1] // c

    def body(i, carry):
        ch = (n_lat - 1 - i) if rev else i
        r0 = pl.multiple_of(ch * c, c)
        o = _delta_chunk(ql_ref[0, pl.ds(r0, c), :], kl_ref[0, pl.ds(r0, c), :], vl_ref[0, pl.ds(r0, c), :],
                         gcl_ref[0, pl.ds(r0, c), :], grl_ref, r0, lane_g0, s_ref, rev)
        ol_ref[0, pl.ds(r0, c), :] = o.astype(ol_ref.dtype)
        return carry

    lax.fori_loop(0, n_lat, body, 0)


def _delta_rule(qk_c, v_c, gcol_c, grow_c, qk_l, v_l, gcol_l, grow_l, rev):
    b, lc, qk2 = qk_c.shape
    ll = qk_l.shape[1]
    n_qh = qk2 // 2 // DN_HEAD
    n_vh = v_c.shape[2] // DN_HEAD
    dh = DN_HEAD
    ng = gcol_c.shape[2]
    kern = functools.partial(_delta_kernel, rev=rev, n_vh=n_vh)

    def seq_specs(l):
        return [
            pl.BlockSpec((1, l, dh), lambda bi, h: (bi, 0, h)),
            pl.BlockSpec((1, l, dh), lambda bi, h: (bi, 0, h + n_qh)),
            pl.BlockSpec((1, l, 2 * dh), lambda bi, h: (bi, 0, h)),
            pl.BlockSpec((1, l, ng), lambda bi, h: (bi, 0, 0)),
            pl.BlockSpec((1, ng, l), lambda bi, h: (bi, 0, 0)),
        ]

    return pl.pallas_call(
        kern,
        out_shape=(jax.ShapeDtypeStruct((b, lc, n_vh * dh), BF16),
                   jax.ShapeDtypeStruct((b, ll, n_vh * dh), BF16)),
        grid=(b, n_qh),
        in_specs=seq_specs(lc) + seq_specs(ll),
        out_specs=(pl.BlockSpec((1, lc, 2 * dh), lambda bi, h: (bi, 0, h)),
                   pl.BlockSpec((1, ll, 2 * dh), lambda bi, h: (bi, 0, h))),
        scratch_shapes=[pltpu.VMEM((2, dh, dh), F32)],
        compiler_params=_cparams(("parallel", "parallel")),
        name="delta_rule_bwd" if rev else "delta_rule_fwd",
    )(qk_c, qk_c, v_c, gcol_c, grow_c, qk_l, qk_l, v_l, gcol_l, grow_l)


def _dn_out_kernel(of_ref, ob_ref, z_ref, og_ref, w_ref, x_ref, gt_ref, o_ref, h_scr):
    @pl.when(pl.program_id(1) == 0)
    def _():
        n_h = h_scr.shape[1] // DN_HEAD
        for hh in range(n_h):
            sl = slice(hh * DN_HEAD, (hh + 1) * DN_HEAD)
            o = of_ref[:, sl].astype(F32) + ob_ref[:, sl].astype(F32)
            ms = jnp.mean(o * o, axis=-1, keepdims=True)
            y = (o * lax.rsqrt(ms + EPS)) * og_ref[...]
            h_scr[:, sl] = (y * _silu(z_ref[:, sl].astype(F32))).astype(BF16)

    y = _dot(h_scr[...], w_ref[...])
    o_ref[...] = x_ref[...] + gt_ref[...] * y


def _dn_out_proj(o_f, o_b, proj, z_off, onorm_g, w_bf, x2, mod3, row_fn, tm):
    m, e = o_f.shape
    d = w_bf.shape[1]
    tn = 512
    zoff = z_off // e
    return pl.pallas_call(
        _dn_out_kernel,
        out_shape=jax.ShapeDtypeStruct((m, d), F32),
        grid=(m // tm, d // tn),
        in_specs=[
            pl.BlockSpec((tm, e), lambda i, j: (i, 0)),
            pl.BlockSpec((tm, e), lambda i, j: (i, 0)),
            pl.BlockSpec((tm, e), lambda i, j: (i, zoff)),
            pl.BlockSpec((1, DN_HEAD), lambda i, j: (0, 0)),
            pl.BlockSpec((e, tn), lambda i, j: (0, j)),
            pl.BlockSpec((tm, tn), lambda i, j: (i, j)),
            pl.BlockSpec((None, 1, tn), lambda i, j: (row_fn(i), 0, 2 * (d // tn) + j)),
        ],
        out_specs=pl.BlockSpec((tm, tn), lambda i, j: (i, j)),
        scratch_shapes=[pltpu.VMEM((tm, e), BF16)],
        compiler_params=_cparams(("parallel", "arbitrary")),
        name="dn_out_proj",
    )(o_f, o_b, proj, onorm_g.reshape(1, DN_HEAD), w_bf, x2, mod3)


def _cv_in_kernel(x_ref, sc_ref, sh_ref, g_ref, wa_ref, wg_ref, ba_ref, bg_ref, o_ref, h_scr):
    @pl.when(pl.program_id(1) == 0)
    def _():
        h_scr[...] = _norm_mod(x_ref[...], g_ref[...], sc_ref[...], sh_ref[...]).astype(BF16)

    h = h_scr[...]
    a = _dot(h, wa_ref[...]) + ba_ref[...]
    gate = _dot(h, wg_ref[...]) + bg_ref[...]
    o_ref[...] = (a * jax.nn.sigmoid(gate)).astype(o_ref.dtype)


def _cv_in_proj(x2, mod3, g, w1_bf, b1, row_fn, tm):
    m, d = x2.shape
    ci = w1_bf.shape[1] // 2
    tn = 1024
    nj = ci // tn
    b1r = b1.reshape(1, 2 * ci)
    return pl.pallas_call(
        _cv_in_kernel,
        out_shape=jax.ShapeDtypeStruct((m, ci), BF16),
        grid=(m // tm, nj),
        in_specs=[
            pl.BlockSpec((tm, d), lambda i, j: (i, 0)),
            _mod_spec(d, 1, row_fn),
            _mod_spec(d, 0, row_fn),
            pl.BlockSpec((1, d), lambda i, j: (0, 0)),
            pl.BlockSpec((d, tn), lambda i, j: (0, j)),
            pl.BlockSpec((d, tn), lambda i, j: (0, j + nj)),
            pl.BlockSpec((1, tn), lambda i, j: (0, j)),
            pl.BlockSpec((1, tn), lambda i, j: (0, j + nj)),
        ],
        out_specs=pl.BlockSpec((tm, tn), lambda i, j: (i, j)),
        scratch_shapes=[pltpu.VMEM((tm, d), BF16)],
        compiler_params=_cparams(("parallel", "arbitrary")),
        name="cv_in_proj",
    )(x2, mod3, mod3, g.reshape(1, d), w1_bf, w1_bf, b1r, b1r)


def _cv_out_kernel(u_ref, up_ref, un_ref, dw_ref, dwb_ref, lg_ref, lb_ref, w_ref, b2_ref, x_ref, gt_ref,
                   o_ref, ext_scr, sh_scr, conv_scr, h_scr, *, tm, taps, rb, cch):
    i = pl.program_id(1)
    n_i = pl.num_programs(1)
    hl = CONV_HALO
    sub = 8
    span = sh_scr.shape[1]

    @pl.when(pl.program_id(2) == 0)
    def _():
        prev = up_ref[0].astype(F32)
        nxt = un_ref[0].astype(F32)
        ext_scr[0:hl, :] = jnp.where(i > 0, prev, 0.0)
        ext_scr[hl:hl + tm, :] = u_ref[0].astype(F32)
        ext_scr[hl + tm:, :] = jnp.where(i < n_i - 1, nxt, 0.0)

        for cc in range(ext_scr.shape[1] // cch):
            lanes = slice(cc * cch, (cc + 1) * cch)
            for r in range(1, sub):
                sh_scr[r - 1] = ext_scr[r:r + span, lanes]

            def rows(rbi, carry):
                r0 = pl.multiple_of(rbi * rb, rb)
                acc = None
                for k in range(taps):
                    off = hl - taps // 2 + k
                    r, a = off % sub, off - off % sub
                    if r == 0:
                        src = ext_scr[pl.ds(r0 + a, rb), lanes]
                    else:
                        src = sh_scr[r - 1, pl.ds(r0 + a, rb), :]
                    term = dw_ref[k:k + 1, lanes] * src
                    acc = term if acc is None else acc + term
                conv_scr[pl.ds(r0, rb), lanes] = acc + dwb_ref[:, lanes]
                return carry

            lax.fori_loop(0, tm // rb, rows, 0)

        def norm_rows(rbi, carry):
            r0 = pl.multiple_of(rbi * rb, rb)
            acc = conv_scr[pl.ds(r0, rb), :]
            mu = jnp.mean(acc, axis=-1, keepdims=True)
            xc = acc - mu
            var = jnp.mean(xc * xc, axis=-1, keepdims=True)
            y = (xc * lax.rsqrt(var + EPS)) * lg_ref[...] + lb_ref[...]
            h_scr[pl.ds(r0, rb), :] = _silu(y).astype(BF16)
            return carry

        lax.fori_loop(0, tm // rb, norm_rows, 0)

    y = _dot(h_scr[...], w_ref[...]) + b2_ref[...]
    o_ref[0] = x_ref[0] + gt_ref[...] * y


def _cv_out_proj(u, dw, dwb, ln_g, ln_b, w2_bf, b2, x3, mod3, batch_row, tm):
    b, l, ci = u.shape
    d = w2_bf.shape[1]
    taps = dw.shape[0]
    tn = 1024
    hl = CONV_HALO
    rpb = tm // hl
    last = l // hl - 1
    cch = min(ci, 512)
    kern = functools.partial(_cv_out_kernel, tm=tm, taps=taps, rb=32, cch=cch)
    vec = lambda n: pl.BlockSpec((1, n), lambda bi, i, j: (0, 0))
    return pl.pallas_call(
        kern,
        out_shape=jax.ShapeDtypeStruct((b, l, d), F32),
        grid=(b, l // tm, d // tn),
        in_specs=[
            pl.BlockSpec((1, tm, ci), lambda bi, i, j: (bi, i, 0)),
            pl.BlockSpec((1, hl, ci), lambda bi, i, j: (bi, jnp.maximum(i * rpb - 1, 0), 0)),
            pl.BlockSpec((1, hl, ci), lambda bi, i, j: (bi, jnp.minimum((i + 1) * rpb, last), 0)),
            pl.BlockSpec((taps, ci), lambda bi, i, j: (0, 0)),
            vec(ci), vec(ci), vec(ci),
            pl.BlockSpec((ci, tn), lambda bi, i, j: (0, j)),
            pl.BlockSpec((1, tn), lambda bi, i, j: (0, j)),
            pl.BlockSpec((1, tm, tn), lambda bi, i, j: (bi, i, j)),
            pl.BlockSpec((None, 1, tn), lambda bi, i, j: (batch_row(bi), 0, 2 * (d // tn) + j)),
        ],
        out_specs=pl.BlockSpec((1, tm, tn), lambda bi, i, j: (bi, i, j)),
        scratch_shapes=[pltpu.VMEM((tm + 2 * hl, ci), F32),
                        pltpu.VMEM((7, tm + 2 * hl - 8, cch), F32),
                        pltpu.VMEM((tm, ci), F32),
                        pltpu.VMEM((tm, ci), BF16)],
        compiler_params=_cparams(("parallel", "parallel", "arbitrary")),
        name="cv_out_proj",
    )(u, u, u, dw, dwb.reshape(1, ci), ln_g.reshape(1, ci), ln_b.reshape(1, ci), w2_bf,
      b2.reshape(1, d), x3, mod3)


def _router_kernel(x_ref, sc_ref, sh_ref, g_ref, w_ref, b_ref, h_ref, eid_ref, wt_ref):
    h = _norm_mod(x_ref[...], g_ref[...], sc_ref[...], sh_ref[...])
    h_ref[...] = h.astype(BF16)
    logits = _dot3(h, w_ref[...]) + b_ref[...]
    lane = lax.broadcasted_iota(jnp.int32, logits.shape, 1)
    neg = jnp.float32(-1e30)
    big = jnp.int32(ROUTER_LANES)
    is_grp = lane < N_GROUPS
    gl = jnp.where(is_grp, logits, neg)
    gmax = jnp.max(gl, axis=-1, keepdims=True)
    gsum = jnp.sum(jnp.where(is_grp, jnp.exp(gl - gmax), 0.0), axis=-1, keepdims=True)
    grp_p = 1.0 / gsum
    gidx = jnp.min(jnp.where(gl == gmax, lane, big), axis=-1, keepdims=True)
    lo = N_GROUPS + gidx * EXPERTS_PER_GROUP
    in_g = jnp.logical_and(lane >= lo, lane < lo + EXPERTS_PER_GROUP)
    el = jnp.where(in_g, logits, neg)
    emax = jnp.max(el, axis=-1, keepdims=True)
    ex = jnp.where(in_g, jnp.exp(el - emax), 0.0)
    prob = jnp.where(in_g, ex / jnp.sum(ex, axis=-1, keepdims=True), -1.0)
    p1 = jnp.max(prob, axis=-1, keepdims=True)
    i1 = jnp.min(jnp.where(prob == p1, lane, big), axis=-1, keepdims=True)
    prob2 = jnp.where(lane == i1, -1.0, prob)
    p2 = jnp.max(prob2, axis=-1, keepdims=True)
    i2 = jnp.min(jnp.where(prob2 == p2, lane, big), axis=-1, keepdims=True)
    denom = p1 + p2
    w1 = grp_p * p1 / denom
    w2 = grp_p * p2 / denom
    eid_ref[...] = jnp.where(lane == 0, i1 - N_GROUPS, jnp.where(lane == 1, i2 - N_GROUPS, 0))
    wt_ref[...] = jnp.where(lane == 0, w1, jnp.where(lane == 1, w2, 0.0))


def _router(x2, mod3, g, w_r, b_r, row_fn, tm):
    m, d = x2.shape
    n = ROUTER_LANES
    return pl.pallas_call(
        _router_kernel,
        out_shape=(jax.ShapeDtypeStruct((m, d), BF16), jax.ShapeDtypeStruct((m, n), jnp.int32),
                   jax.ShapeDtypeStruct((m, n), F32)),
        grid=(m // tm, 1),
        in_specs=[
            pl.BlockSpec((tm, d), lambda i, j: (i, 0)),
            _mod_spec(d, 4, row_fn),
            _mod_spec(d, 3, row_fn),
            pl.BlockSpec((1, d), lambda i, j: (0, 0)),
            pl.BlockSpec((d, n), lambda i, j: (0, 0)),
            pl.BlockSpec((1, n), lambda i, j: (0, 0)),
        ],
        out_specs=(pl.BlockSpec((tm, d), lambda i, j: (i, 0)),
                   pl.BlockSpec((tm, n), lambda i, j: (i, 0)),
                   pl.BlockSpec((tm, n), lambda i, j: (i, 0))),
        compiler_params=_cparams(("parallel", "arbitrary")),
        name="moe_router",
    )(x2, mod3, mod3, g.reshape(1, d), w_r, b_r)


def _expert_kernel(be_ref, nu_ref, x_ref, wgu_ref, wd_ref, o_ref, wgu_bf, wd_bf):
    i = pl.program_id(0)
    prev = be_ref[jnp.maximum(i - 1, 0)]
    changed = jnp.logical_or(i == 0, be_ref[i] != prev)

    @pl.when(changed)
    def _():
        wgu_bf[...] = wgu_ref[...].astype(BF16)
        wd_bf[...] = wd_ref[...].astype(BF16)

    @pl.when(i < nu_ref[0])
    def _():
        de = wd_bf.shape[0]
        gu = _dot(x_ref[...], wgu_bf[...])
        hmid = (_silu(gu[:, :de]) * gu[:, de:]).astype(BF16)
        o_ref[...] = _dot(hmid, wd_bf[...]).astype(o_ref.dtype)

    @pl.when(i >= nu_ref[0])
    def _():
        o_ref[...] = jnp.zeros_like(o_ref)


def _experts(block_expert, n_used, xs, w_gu, w_down):
    n_slots, d = xs.shape
    n_e, _, de2 = w_gu.shape
    de = de2 // 2
    n_blocks = n_slots // MOE_BLOCK
    grid_spec = pltpu.PrefetchScalarGridSpec(
        num_scalar_prefetch=2,
        grid=(n_blocks,),
        in_specs=[
            pl.BlockSpec((MOE_BLOCK, d), lambda i, be, nu: (i, 0)),
            pl.BlockSpec((None, d, de2), lambda i, be, nu: (be[i], 0, 0)),
            pl.BlockSpec((None, de, d), lambda i, be, nu: (be[i], 0, 0)),
        ],
        out_specs=pl.BlockSpec((MOE_BLOCK, d), lambda i, be, nu: (i, 0)),
        scratch_shapes=[pltpu.VMEM((d, de2), BF16), pltpu.VMEM((de, d), BF16)],
    )
    return pl.pallas_call(
        _expert_kernel,
        out_shape=jax.ShapeDtypeStruct((n_slots, d), BF16),
        grid_spec=grid_spec,
        compiler_params=_cparams(("arbitrary",)),
        name="moe_experts",
    )(block_expert, n_used, xs, w_gu, w_down)


def _combine_kernel(x_ref, y0_ref, y1_ref, w_ref, gt_ref, o_ref):
    w = w_ref[...]
    y = w[:, 0:1] * y0_ref[...].astype(F32) + w[:, 1:2] * y1_ref[...].astype(F32)
    o_ref[...] = x_ref[...] + gt_ref[...] * y


def _combine(x2, y0, y1, wts, mod3, row_fn, tm):
    m, d = x2.shape
    return pl.pallas_call(
        _combine_kernel,
        out_shape=jax.ShapeDtypeStruct((m, d), F32),
        grid=(m // tm, 1),
        in_specs=[
            pl.BlockSpec((tm, d), lambda i, j: (i, 0)),
            pl.BlockSpec((tm, d), lambda i, j: (i, 0)),
            pl.BlockSpec((tm, d), lambda i, j: (i, 0)),
            pl.BlockSpec((tm, ROUTER_LANES), lambda i, j: (i, 0)),
            _mod_spec(d, 5, row_fn),
        ],
        out_specs=pl.BlockSpec((tm, d), lambda i, j: (i, 0)),
        compiler_params=_cparams(("parallel", "arbitrary")),
        name="moe_combine",
    )(x2, y0, y1, wts, mod3)


def _final_kernel(x_ref, g_ref, o_ref):
    x = x_ref[...]
    ms = jnp.mean(x * x, axis=-1, keepdims=True)
    o_ref[...] = (x * lax.rsqrt(ms + EPS)) * g_ref[...]


def _final_norm(x2, g, tm):
    m, d = x2.shape
    return pl.pallas_call(
        _final_kernel,
        out_shape=jax.ShapeDtypeStruct((m, d), F32),
        grid=(m // tm,),
        in_specs=[pl.BlockSpec((tm, d), lambda i: (i, 0)), pl.BlockSpec((1, d), lambda i: (0, 0))],
        out_specs=pl.BlockSpec((tm, d), lambda i: (i, 0)),
        compiler_params=_cparams(("parallel",)),
        name="final_norm",
    )(x2, g.reshape(1, d))


def _to_column_major(h):
    b, n, d = h.shape
    rows = n // GRID_W
    return h.reshape(b, rows, GRID_W, d).transpose(0, 2, 1, 3).reshape(b, n, d)


def _from_column_major(h):
    b, n, d = h.shape
    rows = n // GRID_W
    return h.reshape(b, GRID_W, rows, d).transpose(0, 2, 1, 3).reshape(b, n, d)


def _dispatch_tables(eid):
    n_t, top_k = eid.shape
    n_assign = n_t * top_k
    e_flat = eid.reshape(-1)
    order = jnp.argsort(e_flat)
    e_sorted = e_flat[order]
    counts = jnp.zeros((N_EXPERTS,), jnp.int32).at[e_flat].add(1)
    padded = (counts + MOE_BLOCK - 1) // MOE_BLOCK * MOE_BLOCK
    starts = jnp.cumsum(counts) - counts
    pad_ends = jnp.cumsum(padded)
    pad_starts = pad_ends - padded
    slot_sorted = pad_starts[e_sorted] + jnp.arange(n_assign, dtype=jnp.int32) - starts[e_sorted]
    n_blocks = -(-n_assign // MOE_BLOCK) + N_EXPERTS
    n_slots = n_blocks * MOE_BLOCK
    slot_tok = jnp.zeros((n_slots,), jnp.int32).at[slot_sorted].set((order // top_k).astype(jnp.int32))
    slot_of_assign = jnp.zeros((n_assign,), jnp.int32).at[order].set(slot_sorted.astype(jnp.int32))
    n_used = (pad_ends[-1] // MOE_BLOCK).astype(jnp.int32)
    blk = jnp.minimum(jnp.arange(n_blocks, dtype=jnp.int32), n_used - 1)
    block_expert = jnp.minimum(
        jnp.searchsorted(pad_ends, blk * MOE_BLOCK, side="right"), N_EXPERTS - 1).astype(jnp.int32)
    return slot_tok, slot_of_assign.reshape(n_t, top_k), block_expert, n_used.reshape(1)


def _moe(h2, eid, w_gu, w_down):
    slot_tok, slot_of_assign, block_expert, n_used = _dispatch_tables(eid)
    xs = jnp.take(h2, slot_tok, axis=0)
    ys = _experts(block_expert, n_used, xs, w_gu, w_down)
    return jnp.take(ys, slot_of_assign[:, 0], axis=0), jnp.take(ys, slot_of_assign[:, 1], axis=0)


def _gate_lane_params(a_log, dt_bias):
    n_dir, n_h = a_log.shape
    a = jnp.exp(a_log.astype(F32))
    zeros = jnp.zeros_like(a)
    ones = jnp.ones_like(a)
    lane = lambda av, bv: jnp.stack([av, bv], axis=1).reshape(1, n_dir * 2 * n_h)
    isf = jnp.broadcast_to((jnp.arange(n_dir) == 0).astype(F32)[:, None], a.shape)
    return lane(a, zeros), lane(dt_bias.astype(F32), zeros), lane(ones, zeros), lane(isf, isf)


def kernel(x, c, ctx, c_ctx, ada_w, ada_b, norm1_g, norm2_g, dn_w_in, dn_conv_w, dn_a_log, dn_dt_bias, dn_onorm_g, dn_w_out, cv_w1, cv_b1, cv_dw, cv_dwb, cv_ln_g, cv_ln_b, cv_w2, cv_b2, moe_w_grp, moe_b_grp, moe_w_exp, moe_b_exp, moe_w_gu, moe_w_down, final_g):
    bsz, n_lat, d = x.shape
    lc = ctx.shape[1]
    depth = ada_w.shape[0]
    conv_dim = dn_conv_w.shape[2]
    v_dim = dn_w_out.shape[1]
    qk_dim = (conv_dim - v_dim) // 2
    ml, mc = bsz * n_lat, bsz * lc
    tm_l = min(n_lat, 1024)
    tm_c = min(mc, 1024)
    tm_s = min(n_lat, mc, 512)
    ctx_row = bsz
    lat_row = lambda i: (i * tm_l) // n_lat
    lat_row_s = lambda i: (i * tm_s) // n_lat
    ctx_rowf = lambda i: ctx_row

    cvec = jnp.concatenate([c, c_ctx[None, :], jnp.zeros((8 - bsz - 1, d), F32)], axis=0)
    mods = _ada_mod(cvec, ada_w, ada_b)

    xl = x.reshape(ml, d)
    xc = ctx.reshape(mc, d)
    col_major_now = False
    for i in range(depth):
        last = i == depth - 1
        j = i // 2
        mod3 = mods[i].reshape(8, 1, 6 * d)
        col_major = (i // 2) % 2 == 1
        if col_major != col_major_now:
            xl3 = xl.reshape(bsz, n_lat, d)
            xl3 = _to_column_major(xl3) if col_major else _from_column_major(xl3)
            xl = xl3.reshape(ml, d)
            col_major_now = col_major

        if i % 2 == 0:
            w_in = dn_w_in[j]
            w_main = w_in[:, :conv_dim + v_dim].astype(BF16)
            w_ab = w_in[:, conv_dim + v_dim:]
            lanes = _gate_lane_params(dn_a_log[j], dn_dt_bias[j])
            n_ab = w_ab.shape[1]
            seqs = []
            for (x2, l, rf, tm) in ((xc, lc, ctx_rowf, tm_c), (xl, n_lat, lat_row, tm_l)):
                proj, ab = _dn_in_proj(x2, mod3, norm1_g[i], w_main, w_ab, rf, tm)
                proj3 = proj.reshape(bsz, l, conv_dim + v_dim)
                qk = _dn_conv(proj3, dn_conv_w[j], 0, 2 * qk_dim, True, qk_dim)
                v = _dn_conv(proj3, dn_conv_w[j], 2 * qk_dim, v_dim, False, qk_dim)
                gcol = _dn_gates(ab.reshape(bsz, l, n_ab), *lanes)
                grow = jnp.swapaxes(gcol, 1, 2)
                seqs.append((proj, qk, v, gcol, grow))
            (proj_c, qk_c, v_c, gcol_c, grow_c), (proj_l, qk_l, v_l, gcol_l, grow_l) = seqs
            ocf, olf = _delta_rule(qk_c, v_c, gcol_c, grow_c, qk_l, v_l, gcol_l, grow_l, False)
            ocb, olb = _delta_rule(qk_c, v_c, gcol_c, grow_c, qk_l, v_l, gcol_l, grow_l, True)
            w_out = dn_w_out[j].astype(BF16)
            xl = _dn_out_proj(olf.reshape(ml, v_dim), olb.reshape(ml, v_dim), proj_l, conv_dim,
                              dn_onorm_g[j], w_out, xl, mod3, lat_row_s, tm_s)
            xc = _dn_out_proj(ocf.reshape(mc, v_dim), ocb.reshape(mc, v_dim), proj_c, conv_dim,
                              dn_onorm_g[j], w_out, xc, mod3, ctx_rowf, tm_s)
        else:
            w1 = cv_w1[j].astype(BF16)
            w2 = cv_w2[j].astype(BF16)
            ci = w2.shape[0]
            cv_tail = (cv_dw[j], cv_dwb[j], cv_ln_g[j], cv_ln_b[j], w2, cv_b2[j])
            ul = _cv_in_proj(xl, mod3, norm1_g[i], w1, cv_b1[j], lat_row, tm_l)
            xl = _cv_out_proj(ul.reshape(bsz, n_lat, ci), *cv_tail, xl.reshape(bsz, n_lat, d), mod3,
                              lambda bi: bi, min(n_lat, 512)).reshape(ml, d)
            if not last:
                uc = _cv_in_proj(xc, mod3, norm1_g[i], w1, cv_b1[j], ctx_rowf, tm_c)
                xc = _cv_out_proj(uc.reshape(bsz, lc, ci), *cv_tail, xc.reshape(bsz, lc, d), mod3,
                                  lambda bi: ctx_row, lc).reshape(mc, d)

        w_r = jnp.concatenate([moe_w_grp[i], moe_w_exp[i],
                               jnp.zeros((d, ROUTER_LANES - N_GROUPS - N_EXPERTS), F32)], axis=1)
        b_r = jnp.concatenate([moe_b_grp[i], moe_b_exp[i],
                               jnp.zeros((ROUTER_LANES - N_GROUPS - N_EXPERTS,), F32)])[None, :]
        hl2, eid_l, wt_l = _router(xl, mod3, norm2_g[i], w_r, b_r, lat_row, tm_l)
        if last:
            y0, y1 = _moe(hl2, eid_l[:, :2], moe_w_gu[i], moe_w_down[i])
            xl = _combine(xl, y0, y1, wt_l, mod3, lat_row_s, tm_s)
        else:
            hc2, eid_c, wt_c = _router(xc, mod3, norm2_g[i], w_r, b_r, ctx_rowf, tm_c)
            h_all = jnp.concatenate([hc2, hl2], axis=0)
            eid_all = jnp.concatenate([eid_c[:, :2], eid_l[:, :2]], axis=0)
            y0, y1 = _moe(h_all, eid_all, moe_w_gu[i], moe_w_down[i])
            xc = _combine(xc, y0[:mc], y1[:mc], wt_c, mod3, ctx_rowf, tm_s)
            xl = _combine(xl, y0[mc:], y1[mc:], wt_l, mod3, lat_row_s, tm_s)

    out = _final_norm(xl, final_g, tm_l).reshape(bsz, n_lat, d)
    if col_major_now:
        out = _from_column_major(out)
    return out
```

```python
import functools
import math

import jax
import jax.numpy as jnp
from jax import lax
from jax.experimental import pallas as pl
from jax.experimental.pallas import tpu as pltpu

F32 = jnp.float32
BF16 = jnp.bfloat16
EPS = 1e-6

GRID_W = 64
DN_HEAD = 128
DN_CHUNK = 256
INV_BASE = 16
N_GROUPS = 8
EXPERTS_PER_GROUP = 8
N_EXPERTS = N_GROUPS * EXPERTS_PER_GROUP
MOE_BLOCK = 128
ROUTER_LANES = 128
CONV_HALO = 16
V7X_VMEM_LIMIT = 56 * 1024 * 1024


def _cparams(sem, vmem=V7X_VMEM_LIMIT):
    return pltpu.CompilerParams(dimension_semantics=sem, vmem_limit_bytes=vmem)


def _dot(a, b):
    return jnp.dot(a, b, preferred_element_type=F32)


def _dot_nt(a, b):
    return lax.dot_general(a, b, (((1,), (1,)), ((), ())), preferred_element_type=F32)


def _split(x):
    hi = x.astype(BF16)
    lo = (x - hi.astype(F32)).astype(BF16)
    return hi, lo


def _dot3(a, b):
    ah, al = _split(a)
    bh, bl = _split(b)
    return _dot(ah, bh) + _dot(al, bh) + _dot(ah, bl)


def _silu(x):
    return x * jax.nn.sigmoid(x)


def _norm_mod(x, g, sc, sh):
    ms = jnp.mean(x * x, axis=-1, keepdims=True)
    return (x * lax.rsqrt(ms + EPS)) * g * (1.0 + sc) + sh


def _ada_kernel(c_ref, w_ref, b_ref, o_ref):
    s = _silu(c_ref[...])
    o_ref[...] = _dot3(s, w_ref[...]) + b_ref[...]


def _ada_mod(cvec, ada_w, ada_b):
    depth, d, n = ada_w.shape
    tn = 1024
    return pl.pallas_call(
        _ada_kernel,
        out_shape=jax.ShapeDtypeStruct((depth, 8, n), F32),
        grid=(depth, n // tn),
        in_specs=[
            pl.BlockSpec((8, d), lambda l, j: (0, 0)),
            pl.BlockSpec((None, d, tn), lambda l, j: (l, 0, j)),
            pl.BlockSpec((None, 1, tn), lambda l, j: (l, 0, j)),
        ],
        out_specs=pl.BlockSpec((None, 8, tn), lambda l, j: (l, 0, j)),
        compiler_params=_cparams(("parallel", "parallel")),
        name="ada_mod",
    )(cvec, ada_w, ada_b.reshape(depth, 1, n))


def _mod_spec(d, chunk, row_fn):
    return pl.BlockSpec((None, 1, d), lambda i, j: (row_fn(i), 0, chunk))


def _dn_in_kernel(x_ref, sc_ref, sh_ref, g_ref, w_ref, wab_ref, o_ref, ab_ref, h_scr):
    @pl.when(pl.program_id(1) == 0)
    def _():
        h = _norm_mod(x_ref[...], g_ref[...], sc_ref[...], sh_ref[...])
        h_scr[...] = h.astype(BF16)
        ab_ref[...] = _dot3(h, wab_ref[...])

    o_ref[...] = _dot(h_scr[...], w_ref[...]).astype(o_ref.dtype)


def _dn_in_proj(x2, mod3, g, w_bf, wab, row_fn, tm):
    m, d = x2.shape
    n = w_bf.shape[1]
    nab = wab.shape[1]
    tn = 1024
    return pl.pallas_call(
        _dn_in_kernel,
        out_shape=(jax.ShapeDtypeStruct((m, n), BF16), jax.ShapeDtypeStruct((m, nab), F32)),
        grid=(m // tm, n // tn),
        in_specs=[
            pl.BlockSpec((tm, d), lambda i, j: (i, 0)),
            _mod_spec(d, 1, row_fn),
            _mod_spec(d, 0, row_fn),
            pl.BlockSpec((1, d), lambda i, j: (0, 0)),
            pl.BlockSpec((d, tn), lambda i, j: (0, j)),
            pl.BlockSpec((d, nab), lambda i, j: (0, 0)),
        ],
        out_specs=(pl.BlockSpec((tm, tn), lambda i, j: (i, j)),
                   pl.BlockSpec((tm, nab), lambda i, j: (i, 0))),
        scratch_shapes=[pltpu.VMEM((tm, d), BF16)],
        compiler_params=_cparams(("parallel", "arbitrary")),
        name="dn_in_proj",
    )(x2, mod3, mod3, g.reshape(1, d), w_bf, wab)


def _dn_conv_kernel(cur_ref, prev_ref, next_ref, ctx_ref, w_ref, o_ref, ext_scr, *, tl, taps, l2, n_q_blocks):
    c = pl.program_id(1)
    i = pl.program_id(2)
    n_lat_tiles = pl.num_programs(2) - 1
    is_ctx = i == n_lat_tiles
    hl = CONV_HALO
    prev = prev_ref[0].astype(F32)
    nxt = next_ref[0].astype(F32)
    ext_scr[0:hl, :] = jnp.where(jnp.logical_and(i > 0, jnp.logical_not(is_ctx)), prev, 0.0)
    ext_scr[hl:hl + tl, :] = jnp.where(is_ctx, ctx_ref[0], cur_ref[0]).astype(F32)
    ext_scr[hl + tl:, :] = jnp.where(i < n_lat_tiles - 1, nxt, 0.0)
    acc = None
    for k in range(taps):
        term = w_ref[k:k + 1, :] * ext_scr[pl.ds(hl - taps // 2 + k, tl), :]
        acc = term if acc is None else acc + term
    y = _silu(acc)
    cw = y.shape[1]
    if l2:
        scale = jnp.where(c < n_q_blocks, DN_HEAD ** -0.5, 1.0).astype(F32)
        for hh in range(cw // DN_HEAD):
            ys = y[:, hh * DN_HEAD:(hh + 1) * DN_HEAD]
            ss = jnp.sum(ys * ys, axis=-1, keepdims=True)
            o_ref[0, :, hh * DN_HEAD:(hh + 1) * DN_HEAD] = (
                ys * (lax.rsqrt(ss + EPS) * scale)).astype(o_ref.dtype)
    else:
        o_ref[0] = y.astype(o_ref.dtype)


def _dn_conv(proj_l, proj_c, conv_w, ch_off, n_ch, l2, qk_dim):
    b, l, _ = proj_l.shape
    lc = proj_c.shape[1]
    taps = conv_w.shape[0]
    tl = lc
    assert l % tl == 0 and tl % CONV_HALO == 0 and n_ch % 512 == 0 and ch_off % 512 == 0
    cw = 512
    hl = CONV_HALO
    coff = ch_off // cw
    kern = functools.partial(_dn_conv_kernel, tl=tl, taps=taps, l2=l2, n_q_blocks=qk_dim // cw)
    rpb = tl // hl
    n_lt = l // tl
    last = l // hl - 1
    return pl.pallas_call(
        kern,
        out_shape=jax.ShapeDtypeStruct((b, l + lc, n_ch), BF16),
        grid=(b, n_ch // cw, n_lt + 1),
        in_specs=[
            pl.BlockSpec((1, tl, cw), lambda bi, c, i: (bi, jnp.minimum(i, n_lt - 1), c + coff)),
            pl.BlockSpec((1, hl, cw), lambda bi, c, i: (bi, jnp.clip(i * rpb - 1, 0, last), c + coff)),
            pl.BlockSpec((1, hl, cw), lambda bi, c, i: (bi, jnp.minimum((i + 1) * rpb, last), c + coff)),
            pl.BlockSpec((1, tl, cw), lambda bi, c, i: (bi, 0, c + coff)),
            pl.BlockSpec((taps, cw), lambda bi, c, i: (0, c + coff)),
        ],
        out_specs=pl.BlockSpec((1, tl, cw), lambda bi, c, i: (bi, i, c)),
        scratch_shapes=[pltpu.VMEM((tl + 2 * hl, cw), F32)],
        compiler_params=_cparams(("parallel", "parallel", "parallel")),
        name="dn_conv_qk" if l2 else "dn_conv_v",
    )(proj_l, proj_l, proj_l, proj_c, conv_w)


def _dn_gate_kernel(ab_ref, abc_ref, a_ref, dt_ref, isa_ref, isf_ref, o_ref):
    is_ctx = pl.program_id(1) == pl.num_programs(1) - 1
    x = jnp.where(is_ctx, abc_ref[0], ab_ref[0])
    c = x.shape[0]
    z = x + dt_ref[...]
    softplus = jnp.maximum(z, 0.0) + jnp.log(1.0 + jnp.exp(-jnp.abs(z)))
    g = -a_ref[...] * softplus
    beta = jax.nn.sigmoid(x)
    ri = lax.broadcasted_iota(jnp.int32, (c, c), 0)
    ci = lax.broadcasted_iota(jnp.int32, (c, c), 1)
    tri_f = jnp.where(ri >= ci, 1.0, 0.0).astype(BF16)
    tri_b = jnp.where(ri <= ci, 1.0, 0.0).astype(BF16)
    g1 = g.astype(BF16)
    r1 = g - g1.astype(F32)
    g2 = r1.astype(BF16)
    g3 = (r1 - g2.astype(F32)).astype(BF16)
    cum_f = _dot(tri_f, g1) + _dot(tri_f, g2) + _dot(tri_f, g3)
    cum_b = _dot(tri_b, g1) + _dot(tri_b, g2) + _dot(tri_b, g3)
    gc = jnp.where(isf_ref[...] > 0.5, cum_f, cum_b)
    o_ref[0] = jnp.where(isa_ref[...] > 0.5, gc, beta)


def _dn_gates(ab_l, ab_c, a_lane, dt_lane, isa_lane, isf_lane):
    b, l, n = ab_l.shape
    c = DN_CHUNK
    assert ab_c.shape[1] == c and l % c == 0
    n_lt = l // c
    vec = pl.BlockSpec((1, n), lambda bi, i: (0, 0))
    return pl.pallas_call(
        _dn_gate_kernel,
        out_shape=jax.ShapeDtypeStruct((b, l + c, n), F32),
        grid=(b, n_lt + 1),
        in_specs=[pl.BlockSpec((1, c, n), lambda bi, i: (bi, jnp.minimum(i, n_lt - 1), 0)),
                  pl.BlockSpec((1, c, n), lambda bi, i: (bi, 0, 0)), vec, vec, vec, vec],
        out_specs=pl.BlockSpec((1, c, n), lambda bi, i: (bi, i, 0)),
        compiler_params=_cparams(("parallel", "parallel")),
        name="dn_gates",
    )(ab_l, ab_c, a_lane, dt_lane, isa_lane, isf_lane)


def _mm1(a, b):
    return _dot(a.astype(BF16), b.astype(BF16))


def _tri_inverse(ms, in_block, mm):
    c = ms[0].shape[0]
    ri = lax.broadcasted_iota(jnp.int32, (c, c), 0)
    ci = lax.broadcasted_iota(jnp.int32, (c, c), 1)
    eye = jnp.where(ri == ci, 1.0, 0.0).astype(F32)
    mds = [jnp.where(in_block, m, 0.0) for m in ms]
    rests = [m - md for m, md in zip(ms, mds)]
    ts = [eye - md for md in mds]
    ps = [mm(md, md) for md in mds]
    s = 2
    while s < INV_BASE:
        ts = [t + mm(p, t) for p, t in zip(ps, ts)]
        s *= 2
        if s < INV_BASE:
            ps = [mm(p, p) for p in ps]
    ns = [mm(t, rest) for t, rest in zip(ts, rests)]
    rs = [eye - n for n in ns]
    ps = [mm(n, n) for n in ns]
    s = 2
    while s < c // INV_BASE:
        rs = [r + mm(p, r) for p, r in zip(ps, rs)]
        s *= 2
        if s < c // INV_BASE:
            ps = [mm(p, p) for p in ps]
    return [mm(r, t) for r, t in zip(rs, ts)]


def _delta_step(dirs, grow_ref, s_ref):
    c = dirs[0][0].shape[0]
    dh = DN_HEAD
    ri = lax.broadcasted_iota(jnp.int32, (c, c), 0)
    ci = lax.broadcasted_iota(jnp.int32, (c, c), 1)
    sh = jnp.int32(int(math.log2(INV_BASE)))
    in_block = lax.shift_right_logical(ri, sh) == lax.shift_right_logical(ci, sh)
    streams = []
    grams = []
    for d, (q, k, v2, gcol, r0, lane_g0, rev) in enumerate(dirs):
        strict, incl = (ri < ci, ri <= ci) if rev else (ri > ci, ri >= ci)
        lane = lax.broadcasted_iota(jnp.int32, gcol.shape, 1)
        kk = _dot_nt(k, k)
        qk = _dot_nt(q, k)
        grams.append(qk)
        for s in range(2):
            lg = lane_g0 + s
            lb = lg + gcol.shape[1] // 4
            gc_col = jnp.sum(jnp.where(lane == lg, gcol, 0.0), axis=-1, keepdims=True)
            b_col = jnp.sum(jnp.where(lane == lb, gcol, 0.0), axis=-1, keepdims=True)
            gc_row = grow_ref[0, pl.ds(lg, 1), pl.ds(r0, c)]
            decay = jnp.where(incl, jnp.exp(jnp.minimum(gc_col - gc_row, 0.0)), 0.0)
            m = jnp.where(strict, b_col * kk * decay, 0.0)
            streams.append((d, s, gc_col, b_col, decay, m))
    ainvs = _tri_inverse([st[5] for st in streams], in_block, _mm1)

    kqs = []
    for d, (q, k, *_rest) in enumerate(dirs):
        s_cat = jnp.concatenate([s_ref[2 * d], s_ref[2 * d + 1]], axis=1).astype(BF16)
        kqs.append(_dot(jnp.concatenate([k, q], axis=0), s_cat))
    v_news = []
    for (d, s, gc_col, b_col, decay, m), ainv in zip(streams, ainvs):
        v = dirs[d][2][:, s * dh:(s + 1) * dh].astype(F32)
        ks = kqs[d][:c, s * dh:(s + 1) * dh]
        rhs = (b_col * (v - jnp.exp(gc_col) * ks)).astype(BF16)
        ah, al = _split(ainv)
        v_news.append((_dot(ah, rhs) + _dot(al, rhs)).astype(BF16))
    outs = [[None, None] for _ in dirs]
    for (d, s, gc_col, b_col, decay, m), vb in zip(streams, v_news):
        k, rev = dirs[d][1], dirs[d][6]
        qs = kqs[d][c:, s * dh:(s + 1) * dh]
        outs[d][s] = jnp.exp(gc_col) * qs + _dot((grams[d] * decay).astype(BF16), vb)
        last = 0 if rev else c - 1
        gl = gc_col[last:last + 1, :]
        kd = jnp.exp(gl - gc_col) * k.astype(F32)
        s_ref[2 * d + s] = jnp.exp(gl) * s_ref[2 * d + s] + _dot(kd.T.astype(BF16), vb)
    return [jnp.concatenate(o, axis=1) for o in outs]


def _delta_kernel(q_ref, k_ref, v_ref, gcol_ref, grow_ref, o_ref, s_ref, *, n_vh):
    h = pl.program_id(1)
    c = DN_CHUNK
    n_chunks = q_ref.shape[1] // c
    n_lat = n_chunks - 1
    s_ref[...] = jnp.zeros_like(s_ref)
    o_ref[...] = jnp.zeros_like(o_ref)

    def body(i, carry):
        dirs, rows = [], []
        for rev in (False, True):
            lat_ch = (n_lat - i) if rev else (i - 1)
            r0 = pl.multiple_of(jnp.where(i == 0, n_lat, lat_ch) * c, c)
            rows.append(pl.ds(r0, c))
            dirs.append((q_ref[0, rows[-1], :], k_ref[0, rows[-1], :], v_ref[0, rows[-1], :],
                         gcol_ref[0, rows[-1], :], r0, (2 * n_vh if rev else 0) + 2 * h, rev))
        for r, o in zip(rows, _delta_step(dirs, grow_ref, s_ref)):
            o_ref[0, r, :] = o_ref[0, r, :] + o
        return carry

    lax.fori_loop(0, n_chunks, body, 0)


def _delta_rule(qk, v, gcol, grow):
    b, t, qk2 = qk.shape
    dh = DN_HEAD
    n_qh = qk2 // 2 // dh
    n_vh = v.shape[2] // dh
    ng = gcol.shape[2]
    assert t % DN_CHUNK == 0 and n_vh == 2 * n_qh and ng == 4 * n_vh
    kern = functools.partial(_delta_kernel, n_vh=n_vh)
    return pl.pallas_call(
        kern,
        out_shape=jax.ShapeDtypeStruct((b, t, n_vh * dh), F32),
        grid=(b, n_qh),
        in_specs=[
            pl.BlockSpec((1, t, dh), lambda bi, h: (bi, 0, h)),
            pl.BlockSpec((1, t, dh), lambda bi, h: (bi, 0, h + n_qh)),
            pl.BlockSpec((1, t, 2 * dh), lambda bi, h: (bi, 0, h)),
            pl.BlockSpec((1, t, ng), lambda bi, h: (bi, 0, 0)),
            pl.BlockSpec((1, ng, t), lambda bi, h: (bi, 0, 0)),
        ],
        out_specs=pl.BlockSpec((1, t, 2 * dh), lambda bi, h: (bi, 0, h)),
        scratch_shapes=[pltpu.VMEM((4, dh, dh), F32)],
        compiler_params=_cparams(("parallel", "parallel")),
        name="delta_rule",
    )(qk, qk, v, gcol, grow)


def _dn_out_kernel(o_in_ref, z_ref, og_ref, w_ref, x_ref, gt_ref, o_ref, h_scr):
    @pl.when(pl.program_id(2) == 0)
    def _():
        n_h = h_scr.shape[1] // DN_HEAD
        for hh in range(n_h):
            sl = slice(hh * DN_HEAD, (hh + 1) * DN_HEAD)
            o = o_in_ref[0, :, sl]
            ms = jnp.mean(o * o, axis=-1, keepdims=True)
            y = (o * lax.rsqrt(ms + EPS)) * og_ref[...]
            h_scr[:, sl] = (y * _silu(z_ref[0, :, sl].astype(F32))).astype(BF16)

    y = _dot(h_scr[...], w_ref[...])
    o_ref[0] = x_ref[0] + gt_ref[...] * y


def _dn_out_proj(o_all, row_off, proj3, z_off, onorm_g, w_bf, x3, mod3, batch_row, tm):
    b, l, d = x3.shape
    e = o_all.shape[2]
    tn = 512
    assert row_off % tm == 0 and l % tm == 0 and z_off % e == 0 and d % tn == 0
    zoff = z_off // e
    ooff = row_off // tm
    return pl.pallas_call(
        _dn_out_kernel,
        out_shape=jax.ShapeDtypeStruct((b, l, d), F32),
        grid=(b, l // tm, d // tn),
        in_specs=[
            pl.BlockSpec((1, tm, e), lambda bi, i, j: (bi, i + ooff, 0)),
            pl.BlockSpec((1, tm, e), lambda bi, i, j: (bi, i, zoff)),
            pl.BlockSpec((1, DN_HEAD), lambda bi, i, j: (0, 0)),
            pl.BlockSpec((e, tn), lambda bi, i, j: (0, j)),
            pl.BlockSpec((1, tm, tn), lambda bi, i, j: (bi, i, j)),
            pl.BlockSpec((None, 1, tn), lambda bi, i, j: (batch_row(bi), 0, 2 * (d // tn) + j)),
        ],
        out_specs=pl.BlockSpec((1, tm, tn), lambda bi, i, j: (bi, i, j)),
        scratch_shapes=[pltpu.VMEM((tm, e), BF16)],
        compiler_params=_cparams(("parallel", "parallel", "arbitrary")),
        name="dn_out_proj",
    )(o_all, proj3, onorm_g.reshape(1, DN_HEAD), w_bf, x3, mod3)


def _cv_in_kernel(x_ref, sc_ref, sh_ref, g_ref, wa_ref, wg_ref, ba_ref, bg_ref, o_ref, h_scr):
    @pl.when(pl.program_id(1) == 0)
    def _():
        h_scr[...] = _norm_mod(x_ref[...], g_ref[...], sc_ref[...], sh_ref[...]).astype(BF16)

    h = h_scr[...]
    a = _dot(h, wa_ref[...]) + ba_ref[...]
    gate = _dot(h, wg_ref[...]) + bg_ref[...]
    o_ref[...] = (a * jax.nn.sigmoid(gate)).astype(o_ref.dtype)


def _cv_in_proj(x2, mod3, g, w1_bf, b1, row_fn, tm):
    m, d = x2.shape
    ci = w1_bf.shape[1] // 2
    tn = 1024
    nj = ci // tn
    b1r = b1.reshape(1, 2 * ci)
    return pl.pallas_call(
        _cv_in_kernel,
        out_shape=jax.ShapeDtypeStruct((m, ci), BF16),
        grid=(m // tm, nj),
        in_specs=[
            pl.BlockSpec((tm, d), lambda i, j: (i, 0)),
            _mod_spec(d, 1, row_fn),
            _mod_spec(d, 0, row_fn),
            pl.BlockSpec((1, d), lambda i, j: (0, 0)),
            pl.BlockSpec((d, tn), lambda i, j: (0, j)),
            pl.BlockSpec((d, tn), lambda i, j: (0, j + nj)),
            pl.BlockSpec((1, tn), lambda i, j: (0, j)),
            pl.BlockSpec((1, tn), lambda i, j: (0, j + nj)),
        ],
        out_specs=pl.BlockSpec((tm, tn), lambda i, j: (i, j)),
        scratch_shapes=[pltpu.VMEM((tm, d), BF16)],
        compiler_params=_cparams(("parallel", "arbitrary")),
        name="cv_in_proj",
    )(x2, mod3, mod3, g.reshape(1, d), w1_bf, w1_bf, b1r, b1r)


def _cv_out_kernel(u_ref, up_ref, un_ref, dw_ref, dwb_ref, lg_ref, lb_ref, w_ref, b2_ref, x_ref, gt_ref,
                   o_ref, ext_scr, sh_scr, conv_scr, h_scr, *, tm, taps, rb, cch):
    i = pl.program_id(1)
    n_i = pl.num_programs(1)
    hl = CONV_HALO
    sub = 8
    span = sh_scr.shape[1]

    @pl.when(pl.program_id(2) == 0)
    def _():
        prev = up_ref[0].astype(F32)
        nxt = un_ref[0].astype(F32)
        ext_scr[0:hl, :] = jnp.where(i > 0, prev, 0.0)
        ext_scr[hl:hl + tm, :] = u_ref[0].astype(F32)
        ext_scr[hl + tm:, :] = jnp.where(i < n_i - 1, nxt, 0.0)

        for cc in range(ext_scr.shape[1] // cch):
            lanes = slice(cc * cch, (cc + 1) * cch)
            for r in range(1, sub):
                sh_scr[r - 1] = ext_scr[r:r + span, lanes]

            def rows(rbi, carry):
                r0 = pl.multiple_of(rbi * rb, rb)
                acc = None
                for k in range(taps):
                    off = hl - taps // 2 + k
                    r, a = off % sub, off - off % sub
                    if r == 0:
                        src = ext_scr[pl.ds(r0 + a, rb), lanes]
                    else:
                        src = sh_scr[r - 1, pl.ds(r0 + a, rb), :]
                    term = dw_ref[k:k + 1, lanes] * src
                    acc = term if acc is None else acc + term
                conv_scr[pl.ds(r0, rb), lanes] = acc + dwb_ref[:, lanes]
                return carry

            lax.fori_loop(0, tm // rb, rows, 0)

        def norm_rows(rbi, carry):
            r0 = pl.multiple_of(rbi * rb, rb)
            acc = conv_scr[pl.ds(r0, rb), :]
            mu = jnp.mean(acc, axis=-1, keepdims=True)
            xc = acc - mu
            var = jnp.mean(xc * xc, axis=-1, keepdims=True)
            y = (xc * lax.rsqrt(var + EPS)) * lg_ref[...] + lb_ref[...]
            h_scr[pl.ds(r0, rb), :] = _silu(y).astype(BF16)
            return carry

        lax.fori_loop(0, tm // rb, norm_rows, 0)

    y = _dot(h_scr[...], w_ref[...]) + b2_ref[...]
    o_ref[0] = x_ref[0] + gt_ref[...] * y


def _cv_out_proj(u, dw, dwb, ln_g, ln_b, w2_bf, b2, x3, mod3, batch_row, tm):
    b, l, ci = u.shape
    d = w2_bf.shape[1]
    taps = dw.shape[0]
    tn = 1024
    hl = CONV_HALO
    rpb = tm // hl
    last = l // hl - 1
    cch = min(ci, 512)
    kern = functools.partial(_cv_out_kernel, tm=tm, taps=taps, rb=32, cch=cch)
    vec = lambda n: pl.BlockSpec((1, n), lambda bi, i, j: (0, 0))
    return pl.pallas_call(
        kern,
        out_shape=jax.ShapeDtypeStruct((b, l, d), F32),
        grid=(b, l // tm, d // tn),
        in_specs=[
            pl.BlockSpec((1, tm, ci), lambda bi, i, j: (bi, i, 0)),
            pl.BlockSpec((1, hl, ci), lambda bi, i, j: (bi, jnp.maximum(i * rpb - 1, 0), 0)),
            pl.BlockSpec((1, hl, ci), lambda bi, i, j: (bi, jnp.minimum((i + 1) * rpb, last), 0)),
            pl.BlockSpec((taps, ci), lambda bi, i, j: (0, 0)),
            vec(ci), vec(ci), vec(ci),
            pl.BlockSpec((ci, tn), lambda bi, i, j: (0, j)),
            pl.BlockSpec((1, tn), lambda bi, i, j: (0, j)),
            pl.BlockSpec((1, tm, tn), lambda bi, i, j: (bi, i, j)),
            pl.BlockSpec((None, 1, tn), lambda bi, i, j: (batch_row(bi), 0, 2 * (d // tn) + j)),
        ],
        out_specs=pl.BlockSpec((1, tm, tn), lambda bi, i, j: (bi, i, j)),
        scratch_shapes=[pltpu.VMEM((tm + 2 * hl, ci), F32),
                        pltpu.VMEM((7, tm + 2 * hl - 8, cch), F32),
                        pltpu.VMEM((tm, ci), F32),
                        pltpu.VMEM((tm, ci), BF16)],
        compiler_params=_cparams(("parallel", "parallel", "arbitrary")),
        name="cv_out_proj",
    )(u, u, u, dw, dwb.reshape(1, ci), ln_g.reshape(1, ci), ln_b.reshape(1, ci), w2_bf,
      b2.reshape(1, d), x3, mod3)


def _router_kernel(x_ref, sc_ref, sh_ref, g_ref, w_ref, b_ref, h_ref, eid_ref, wt_ref):
    h = _norm_mod(x_ref[...], g_ref[...], sc_ref[...], sh_ref[...])
    h_ref[...] = h.astype(BF16)
    logits = _dot3(h, w_ref[...]) + b_ref[...]
    lane = lax.broadcasted_iota(jnp.int32, logits.shape, 1)
    neg = jnp.float32(-1e30)
    big = jnp.int32(ROUTER_LANES)
    is_grp = lane < N_GROUPS
    gl = jnp.where(is_grp, logits, neg)
    gmax = jnp.max(gl, axis=-1, keepdims=True)
    gsum = jnp.sum(jnp.where(is_grp, jnp.exp(gl - gmax), 0.0), axis=-1, keepdims=True)
    grp_p = 1.0 / gsum
    gidx = jnp.min(jnp.where(gl == gmax, lane, big), axis=-1, keepdims=True)
    lo = N_GROUPS + gidx * EXPERTS_PER_GROUP
    in_g = jnp.logical_and(lane >= lo, lane < lo + EXPERTS_PER_GROUP)
    el = jnp.where(in_g, logits, neg)
    emax = jnp.max(el, axis=-1, keepdims=True)
    ex = jnp.where(in_g, jnp.exp(el - emax), 0.0)
    prob = jnp.where(in_g, ex / jnp.sum(ex, axis=-1, keepdims=True), -1.0)
    p1 = jnp.max(prob, axis=-1, keepdims=True)
    i1 = jnp.min(jnp.where(prob == p1, lane, big), axis=-1, keepdims=True)
    prob2 = jnp.where(lane == i1, -1.0, prob)
    p2 = jnp.max(prob2, axis=-1, keepdims=True)
    i2 = jnp.min(jnp.where(prob2 == p2, lane, big), axis=-1, keepdims=True)
    denom = p1 + p2
    w1 = grp_p * p1 / denom
    w2 = grp_p * p2 / denom
    eid_ref[...] = jnp.where(lane == 0, i1 - N_GROUPS, jnp.where(lane == 1, i2 - N_GROUPS, 0))
    wt_ref[...] = jnp.where(lane == 0, w1, jnp.where(lane == 1, w2, 0.0))


def _router(x2, mod3, g, w_r, b_r, row_fn, tm):
    m, d = x2.shape
    n = ROUTER_LANES
    return pl.pallas_call(
        _router_kernel,
        out_shape=(jax.ShapeDtypeStruct((m, d), BF16), jax.ShapeDtypeStruct((m, n), jnp.int32),
                   jax.ShapeDtypeStruct((m, n), F32)),
        grid=(m // tm, 1),
        in_specs=[
            pl.BlockSpec((tm, d), lambda i, j: (i, 0)),
            _mod_spec(d, 4, row_fn),
            _mod_spec(d, 3, row_fn),
            pl.BlockSpec((1, d), lambda i, j: (0, 0)),
            pl.BlockSpec((d, n), lambda i, j: (0, 0)),
            pl.BlockSpec((1, n), lambda i, j: (0, 0)),
        ],
        out_specs=(pl.BlockSpec((tm, d), lambda i, j: (i, 0)),
                   pl.BlockSpec((tm, n), lambda i, j: (i, 0)),
                   pl.BlockSpec((tm, n), lambda i, j: (i, 0))),
        compiler_params=_cparams(("parallel", "arbitrary")),
        name="moe_router",
    )(x2, mod3, mod3, g.reshape(1, d), w_r, b_r)


def _expert_kernel(be_ref, nu_ref, x_ref, wgu_ref, wd_ref, o_ref, wgu_bf, wd_bf):
    i = pl.program_id(0)
    prev = be_ref[jnp.maximum(i - 1, 0)]
    changed = jnp.logical_or(i == 0, be_ref[i] != prev)

    @pl.when(changed)
    def _():
        wgu_bf[...] = wgu_ref[...].astype(BF16)
        wd_bf[...] = wd_ref[...].astype(BF16)

    @pl.when(i < nu_ref[0])
    def _():
        de = wd_bf.shape[0]
        gu = _dot(x_ref[...], wgu_bf[...])
        hmid = (_silu(gu[:, :de]) * gu[:, de:]).astype(BF16)
        o_ref[...] = _dot(hmid, wd_bf[...]).astype(o_ref.dtype)

    @pl.when(i >= nu_ref[0])
    def _():
        o_ref[...] = jnp.zeros_like(o_ref)


def _experts(block_expert, n_used, xs, w_gu, w_down, layer):
    n_slots, d = xs.shape
    de2 = w_gu.shape[3]
    de = de2 // 2
    n_blocks = n_slots // MOE_BLOCK
    grid_spec = pltpu.PrefetchScalarGridSpec(
        num_scalar_prefetch=2,
        grid=(n_blocks,),
        in_specs=[
            pl.BlockSpec((MOE_BLOCK, d), lambda i, be, nu: (i, 0)),
            pl.BlockSpec((None, None, d, de2), lambda i, be, nu: (layer, be[i], 0, 0)),
            pl.BlockSpec((None, None, de, d), lambda i, be, nu: (layer, be[i], 0, 0)),
        ],
        out_specs=pl.BlockSpec((MOE_BLOCK, d), lambda i, be, nu: (i, 0)),
        scratch_shapes=[pltpu.VMEM((d, de2), BF16), pltpu.VMEM((de, d), BF16)],
    )
    return pl.pallas_call(
        _expert_kernel,
        out_shape=jax.ShapeDtypeStruct((n_slots, d), BF16),
        grid_spec=grid_spec,
        compiler_params=_cparams(("arbitrary",)),
        name="moe_experts",
    )(block_expert, n_used, xs, w_gu, w_down)


def _combine_kernel(x_ref, y0_ref, y1_ref, w_ref, gt_ref, o_ref):
    w = w_ref[...]
    y = w[:, 0:1] * y0_ref[...].astype(F32) + w[:, 1:2] * y1_ref[...].astype(F32)
    o_ref[...] = x_ref[...] + gt_ref[...] * y


def _combine(x2, y0, y1, y_row_off, wts, mod3, row_fn, tm):
    m, d = x2.shape
    assert y_row_off % tm == 0 and m % tm == 0
    yoff = y_row_off // tm
    return pl.pallas_call(
        _combine_kernel,
        out_shape=jax.ShapeDtypeStruct((m, d), F32),
        grid=(m // tm, 1),
        in_specs=[
            pl.BlockSpec((tm, d), lambda i, j: (i, 0)),
            pl.BlockSpec((tm, d), lambda i, j: (i + yoff, 0)),
            pl.BlockSpec((tm, d), lambda i, j: (i + yoff, 0)),
            pl.BlockSpec((tm, ROUTER_LANES), lambda i, j: (i, 0)),
            _mod_spec(d, 5, row_fn),
        ],
        out_specs=pl.BlockSpec((tm, d), lambda i, j: (i, 0)),
        compiler_params=_cparams(("parallel", "arbitrary")),
        name="moe_combine",
    )(x2, y0, y1, wts, mod3)


def _final_kernel(x_ref, g_ref, o_ref):
    x = x_ref[...]
    ms = jnp.mean(x * x, axis=-1, keepdims=True)
    o_ref[...] = (x * lax.rsqrt(ms + EPS)) * g_ref[...]


def _final_norm(x2, g, tm):
    m, d = x2.shape
    return pl.pallas_call(
        _final_kernel,
        out_shape=jax.ShapeDtypeStruct((m, d), F32),
        grid=(m // tm,),
        in_specs=[pl.BlockSpec((tm, d), lambda i: (i, 0)), pl.BlockSpec((1, d), lambda i: (0, 0))],
        out_specs=pl.BlockSpec((tm, d), lambda i: (i, 0)),
        compiler_params=_cparams(("parallel",)),
        name="final_norm",
    )(x2, g.reshape(1, d))


def _to_column_major(h):
    b, n, d = h.shape
    rows = n // GRID_W
    return h.reshape(b, rows, GRID_W, d).transpose(0, 2, 1, 3).reshape(b, n, d)


def _from_column_major(h):
    b, n, d = h.shape
    rows = n // GRID_W
    return h.reshape(b, GRID_W, rows, d).transpose(0, 2, 1, 3).reshape(b, n, d)


def _dispatch_tables(eid):
    n_t, top_k = eid.shape
    n_assign = n_t * top_k
    i32 = jnp.int32
    e_flat = eid.reshape(-1)
    order = jnp.argsort(e_flat).astype(i32)
    rank = jnp.argsort(order).astype(i32)
    onehot = e_flat[:, None] == jnp.arange(N_EXPERTS, dtype=i32)[None, :]
    counts = jnp.sum(onehot.astype(i32), axis=0)
    padded = (counts + MOE_BLOCK - 1) // MOE_BLOCK * MOE_BLOCK
    starts = jnp.cumsum(counts) - counts
    pad_ends = jnp.cumsum(padded)
    pad_starts = pad_ends - padded
    lookup = lambda table: jnp.sum(jnp.where(onehot, table[None, :], 0), axis=1)
    slot_of_assign = lookup(pad_starts - starts) + rank
    n_blocks = -(-n_assign // MOE_BLOCK) + N_EXPERTS
    n_used = (pad_ends[-1] // MOE_BLOCK).astype(i32)
    blk = jnp.arange(n_blocks, dtype=i32)
    blk_c = jnp.minimum(blk, n_used - 1)
    block_expert = jnp.minimum(
        jnp.sum((pad_ends[None, :] <= (blk_c * MOE_BLOCK)[:, None]).astype(i32), axis=1), N_EXPERTS - 1)
    be_hot = block_expert[:, None] == jnp.arange(N_EXPERTS, dtype=i32)[None, :]
    blk_lookup = lambda table: jnp.sum(jnp.where(be_hot, table[None, :], 0), axis=1)
    row0 = blk_c * MOE_BLOCK - blk_lookup(pad_starts)
    n_valid = jnp.where(blk < n_used, blk_lookup(counts) - row0, 0)
    r = jnp.arange(MOE_BLOCK, dtype=i32)[None, :]
    pos = (blk_lookup(starts) + row0)[:, None] + r
    valid = r < n_valid[:, None]
    src = jnp.take(order, jnp.clip(pos, 0, n_assign - 1).reshape(-1), mode="clip")
    slot_tok = jnp.where(valid.reshape(-1), src // top_k, 0)
    return slot_tok, slot_of_assign.reshape(n_t, top_k), block_expert, n_used.reshape(1)


def _moe(h2, eid, w_gu, w_down, layer):
    slot_tok, slot_of_assign, block_expert, n_used = _dispatch_tables(eid)
    xs = jnp.take(h2, slot_tok, axis=0, mode="clip")
    ys = _experts(block_expert, n_used, xs, w_gu, w_down, layer)
    return (jnp.take(ys, slot_of_assign[:, 0], axis=0, mode="clip"),
            jnp.take(ys, slot_of_assign[:, 1], axis=0, mode="clip"))


def _gate_lane_params(a_log, dt_bias):
    n_dir, n_h = a_log.shape
    a = jnp.exp(a_log.astype(F32))
    zeros = jnp.zeros_like(a)
    ones = jnp.ones_like(a)
    lane = lambda av, bv: jnp.stack([av, bv], axis=1).reshape(1, n_dir * 2 * n_h)
    isf = jnp.broadcast_to((jnp.arange(n_dir) == 0).astype(F32)[:, None], a.shape)
    return lane(a, zeros), lane(dt_bias.astype(F32), zeros), lane(ones, zeros), lane(isf, isf)


def kernel(x, c, ctx, c_ctx, ada_w, ada_b, norm1_g, norm2_g, dn_w_in, dn_conv_w, dn_a_log, dn_dt_bias, dn_onorm_g, dn_w_out, cv_w1, cv_b1, cv_dw, cv_dwb, cv_ln_g, cv_ln_b, cv_w2, cv_b2, moe_w_grp, moe_b_grp, moe_w_exp, moe_b_exp, moe_w_gu, moe_w_down, final_g):
    bsz, n_lat, d = x.shape
    lc = ctx.shape[1]
    depth = ada_w.shape[0]
    conv_dim = dn_conv_w.shape[2]
    v_dim = dn_w_out.shape[1]
    qk_dim = (conv_dim - v_dim) // 2
    ml, mc = bsz * n_lat, bsz * lc
    tm_l = min(n_lat, 1024)
    tm_c = min(mc, 1024)
    tm_s = min(n_lat, mc, 512)
    ctx_row = bsz
    lat_row = lambda i: (i * tm_l) // n_lat
    lat_row_s = lambda i: (i * tm_s) // n_lat
    ctx_rowf = lambda i: ctx_row

    cvec = jnp.concatenate([c, c_ctx[None, :], jnp.zeros((8 - bsz - 1, d), F32)], axis=0)
    mods = _ada_mod(cvec, ada_w, ada_b)

    xl = x.reshape(ml, d)
    xc = ctx.reshape(mc, d)
    col_major_now = False
    for i in range(depth):
        last = i == depth - 1
        j = i // 2
        mod3 = mods[i].reshape(8, 1, 6 * d)
        col_major = (i // 2) % 2 == 1
        if col_major != col_major_now:
            xl3 = xl.reshape(bsz, n_lat, d)
            xl3 = _to_column_major(xl3) if col_major else _from_column_major(xl3)
            xl = xl3.reshape(ml, d)
            col_major_now = col_major

        if i % 2 == 0:
            w_in = dn_w_in[j]
            w_main = w_in[:, :conv_dim + v_dim].astype(BF16)
            w_ab = w_in[:, conv_dim + v_dim:]
            lanes = _gate_lane_params(dn_a_log[j], dn_dt_bias[j])
            n_ab = w_ab.shape[1]
            n_proj = conv_dim + v_dim
            proj_c, ab_c = _dn_in_proj(xc, mod3, norm1_g[i], w_main, w_ab, ctx_rowf, tm_c)
            proj_l, ab_l = _dn_in_proj(xl, mod3, norm1_g[i], w_main, w_ab, lat_row, tm_l)
            proj_c = proj_c.reshape(bsz, lc, n_proj)
            proj_l = proj_l.reshape(bsz, n_lat, n_proj)
            qk = _dn_conv(proj_l, proj_c, dn_conv_w[j], 0, 2 * qk_dim, True, qk_dim)
            v = _dn_conv(proj_l, proj_c, dn_conv_w[j], 2 * qk_dim, v_dim, False, qk_dim)
            gcol = _dn_gates(ab_l.reshape(bsz, n_lat, n_ab), ab_c.reshape(bsz, lc, n_ab), *lanes)
            o_all = _delta_rule(qk, v, gcol, jnp.swapaxes(gcol, 1, 2))
            w_out = dn_w_out[j].astype(BF16)
            xl = _dn_out_proj(o_all, 0, proj_l, conv_dim, dn_onorm_g[j], w_out, xl.reshape(bsz, n_lat, d),
                              mod3, lambda bi: bi, tm_s).reshape(ml, d)
            xc = _dn_out_proj(o_all, n_lat, proj_c, conv_dim, dn_onorm_g[j], w_out, xc.reshape(bsz, lc, d),
                              mod3, lambda bi: ctx_row, lc).reshape(mc, d)
        else:
            w1 = cv_w1[j].astype(BF16)
            w2 = cv_w2[j].astype(BF16)
            ci = w2.shape[0]
            cv_tail = (cv_dw[j], cv_dwb[j], cv_ln_g[j], cv_ln_b[j], w2, cv_b2[j])
            ul = _cv_in_proj(xl, mod3, norm1_g[i], w1, cv_b1[j], lat_row, tm_l)
            xl = _cv_out_proj(ul.reshape(bsz, n_lat, ci), *cv_tail, xl.reshape(bsz, n_lat, d), mod3,
                              lambda bi: bi, min(n_lat, 512)).reshape(ml, d)
            if not last:
                uc = _cv_in_proj(xc, mod3, norm1_g[i], w1, cv_b1[j], ctx_rowf, tm_c)
                xc = _cv_out_proj(uc.reshape(bsz, lc, ci), *cv_tail, xc.reshape(bsz, lc, d), mod3,
                                  lambda bi: ctx_row, lc).reshape(mc, d)

        w_r = jnp.concatenate([moe_w_grp[i], moe_w_exp[i],
                               jnp.zeros((d, ROUTER_LANES - N_GROUPS - N_EXPERTS), F32)], axis=1)
        b_r = jnp.concatenate([moe_b_grp[i], moe_b_exp[i],
                               jnp.zeros((ROUTER_LANES - N_GROUPS - N_EXPERTS,), F32)])[None, :]
        hl2, eid_l, wt_l = _router(xl, mod3, norm2_g[i], w_r, b_r, lat_row, tm_l)
        if last:
            y0, y1 = _moe(hl2, eid_l[:, :2], moe_w_gu, moe_w_down, i)
            xl = _combine(xl, y0, y1, 0, wt_l, mod3, lat_row_s, tm_s)
        else:
            hc2, eid_c, wt_c = _router(xc, mod3, norm2_g[i], w_r, b_r, ctx_rowf, tm_c)
            h_all = jnp.concatenate([hc2, hl2], axis=0)
            eid_all = jnp.concatenate([eid_c[:, :2], eid_l[:, :2]], axis=0)
            y0, y1 = _moe(h_all, eid_all, moe_w_gu, moe_w_down, i)
            xc = _combine(xc, y0, y1, 0, wt_c, mod3, ctx_rowf, tm_s)
            xl = _combine(xl, y0, y1, mc, wt_l, mod3, lat_row_s, tm_s)

    out = _final_norm(xl, final_g, tm_l).reshape(bsz, n_lat, d)
    if col_major_now:
        out = _from_column_major(out)
    return out
```

```python
import functools
import math

import jax
import jax.numpy as jnp
from jax import lax
from jax.experimental import pallas as pl
from jax.experimental.pallas import tpu as pltpu

F32 = jnp.float32
BF16 = jnp.bfloat16
EPS = 1e-6

GRID_W = 64
DN_HEAD = 128
DN_CHUNK = 256
INV_BASE = 16
N_GROUPS = 8
EXPERTS_PER_GROUP = 8
N_EXPERTS = N_GROUPS * EXPERTS_PER_GROUP
MOE_BLOCK = 256
ROUTER_LANES = 128
CONV_HALO = 16
V7X_VMEM_LIMIT = 56 * 1024 * 1024


def _cparams(sem, vmem=V7X_VMEM_LIMIT):
    return pltpu.CompilerParams(dimension_semantics=sem, vmem_limit_bytes=vmem)


def _dot(a, b):
    return jnp.dot(a, b, preferred_element_type=F32)


def _dot_nt(a, b):
    return lax.dot_general(a, b, (((1,), (1,)), ((), ())), preferred_element_type=F32)


def _split(x):
    hi = x.astype(BF16)
    lo = (x - hi.astype(F32)).astype(BF16)
    return hi, lo


def _dot3(a, b):
    ah, al = _split(a)
    bh, bl = _split(b)
    return _dot(ah, bh) + _dot(al, bh) + _dot(ah, bl)


def _silu(x):
    return x * jax.nn.sigmoid(x)


_HI16 = 0xFFFF0000


def _pack_bf16_pairs(x):
    half = x.shape[1] // 2
    bits = lax.bitcast_convert_type(x.astype(BF16).astype(F32), jnp.uint32)
    return (bits[:, :half] >> 16) | (bits[:, half:] & jnp.uint32(_HI16))


def _unpack_bf16_pairs(w):
    lo = lax.bitcast_convert_type(w << 16, F32)
    hi = lax.bitcast_convert_type(w & jnp.uint32(_HI16), F32)
    return jnp.concatenate([lo, hi], axis=1)


def _norm_mod(x, g, sc, sh):
    ms = jnp.mean(x * x, axis=-1, keepdims=True)
    return (x * lax.rsqrt(ms + EPS)) * g * (1.0 + sc) + sh


def _ada_kernel(c_ref, w_ref, b_ref, o_ref):
    s = _silu(c_ref[...])
    o_ref[...] = _dot3(s, w_ref[...]) + b_ref[...]


def _ada_mod(cvec, ada_w, ada_b):
    depth, d, n = ada_w.shape
    tn = 1024
    return pl.pallas_call(
        _ada_kernel,
        out_shape=jax.ShapeDtypeStruct((depth, 8, n), F32),
        grid=(depth, n // tn),
        in_specs=[
            pl.BlockSpec((8, d), lambda l, j: (0, 0)),
            pl.BlockSpec((None, d, tn), lambda l, j: (l, 0, j)),
            pl.BlockSpec((None, 1, tn), lambda l, j: (l, 0, j)),
        ],
        out_specs=pl.BlockSpec((None, 8, tn), lambda l, j: (l, 0, j)),
        compiler_params=_cparams(("parallel", "parallel")),
        name="ada_mod",
    )(cvec, ada_w, ada_b.reshape(depth, 1, n))


def _mod_spec(d, chunk, row_fn):
    return pl.BlockSpec((None, 1, d), lambda i, j: (row_fn(i), 0, chunk))


def _dn_in_kernel(x_ref, sc_ref, sh_ref, g_ref, w_ref, wab_ref, o_ref, ab_ref, h_scr):
    @pl.when(pl.program_id(1) == 0)
    def _():
        h = _norm_mod(x_ref[...], g_ref[...], sc_ref[...], sh_ref[...])
        h_scr[...] = h.astype(BF16)
        ab_ref[...] = _dot3(h, wab_ref[...])

    o_ref[...] = _dot(h_scr[...], w_ref[...]).astype(o_ref.dtype)


def _dn_in_proj(x2, mod3, g, w_bf, wab, row_fn, tm):
    m, d = x2.shape
    n = w_bf.shape[1]
    nab = wab.shape[1]
    tn = 1024
    return pl.pallas_call(
        _dn_in_kernel,
        out_shape=(jax.ShapeDtypeStruct((m, n), BF16), jax.ShapeDtypeStruct((m, nab), F32)),
        grid=(m // tm, n // tn),
        in_specs=[
            pl.BlockSpec((tm, d), lambda i, j: (i, 0)),
            _mod_spec(d, 1, row_fn),
            _mod_spec(d, 0, row_fn),
            pl.BlockSpec((1, d), lambda i, j: (0, 0)),
            pl.BlockSpec((d, tn), lambda i, j: (0, j)),
            pl.BlockSpec((d, nab), lambda i, j: (0, 0)),
        ],
        out_specs=(pl.BlockSpec((tm, tn), lambda i, j: (i, j)),
                   pl.BlockSpec((tm, nab), lambda i, j: (i, 0))),
        scratch_shapes=[pltpu.VMEM((tm, d), BF16)],
        compiler_params=_cparams(("parallel", "arbitrary")),
        name="dn_in_proj",
    )(x2, mod3, mod3, g.reshape(1, d), w_bf, wab)


def _dn_conv_kernel(cur_ref, prev_ref, next_ref, ctx_ref, w_ref, o_ref, ext_scr, *, tl, taps, l2, n_q_heads):
    c = pl.program_id(1)
    i = pl.program_id(2)
    n_lat_tiles = pl.num_programs(2) - 1
    is_ctx = i == n_lat_tiles
    hl = CONV_HALO
    prev = prev_ref[0].astype(F32)
    nxt = next_ref[0].astype(F32)
    ext_scr[0:hl, :] = jnp.where(jnp.logical_and(i > 0, jnp.logical_not(is_ctx)), prev, 0.0)
    ext_scr[hl:hl + tl, :] = jnp.where(is_ctx, ctx_ref[0], cur_ref[0]).astype(F32)
    ext_scr[hl + tl:, :] = jnp.where(i < n_lat_tiles - 1, nxt, 0.0)
    acc = None
    for k in range(taps):
        term = w_ref[k:k + 1, :] * ext_scr[pl.ds(hl - taps // 2 + k, tl), :]
        acc = term if acc is None else acc + term
    y = _silu(acc)
    cw = y.shape[1]
    if l2:
        for hh in range(cw // DN_HEAD):
            is_q = c * (cw // DN_HEAD) + hh < n_q_heads
            scale = jnp.where(is_q, DN_HEAD ** -0.5, 1.0).astype(F32)
            ys = y[:, hh * DN_HEAD:(hh + 1) * DN_HEAD]
            ss = jnp.sum(ys * ys, axis=-1, keepdims=True)
            o_ref[0, :, hh * DN_HEAD:(hh + 1) * DN_HEAD] = (
                ys * (lax.rsqrt(ss + EPS) * scale)).astype(o_ref.dtype)
    else:
        o_ref[0] = y.astype(o_ref.dtype)


def _dn_conv(proj_l, proj_c, conv_w, ch_off, n_ch, l2, qk_dim):
    b, l, _ = proj_l.shape
    lc = proj_c.shape[1]
    taps = conv_w.shape[0]
    tl = lc
    cw = 1024
    assert l % tl == 0 and tl % CONV_HALO == 0 and n_ch % cw == 0 and ch_off % cw == 0
    hl = CONV_HALO
    coff = ch_off // cw
    kern = functools.partial(_dn_conv_kernel, tl=tl, taps=taps, l2=l2, n_q_heads=qk_dim // DN_HEAD)
    rpb = tl // hl
    n_lt = l // tl
    last = l // hl - 1
    return pl.pallas_call(
        kern,
        out_shape=jax.ShapeDtypeStruct((b, l + lc, n_ch), BF16),
        grid=(b, n_ch // cw, n_lt + 1),
        in_specs=[
            pl.BlockSpec((1, tl, cw), lambda bi, c, i: (bi, jnp.minimum(i, n_lt - 1), c + coff)),
            pl.BlockSpec((1, hl, cw), lambda bi, c, i: (bi, jnp.clip(i * rpb - 1, 0, last), c + coff)),
            pl.BlockSpec((1, hl, cw), lambda bi, c, i: (bi, jnp.minimum((i + 1) * rpb, last), c + coff)),
            pl.BlockSpec((1, tl, cw), lambda bi, c, i: (bi, 0, c + coff)),
            pl.BlockSpec((taps, cw), lambda bi, c, i: (0, c + coff)),
        ],
        out_specs=pl.BlockSpec((1, tl, cw), lambda bi, c, i: (bi, i, c)),
        scratch_shapes=[pltpu.VMEM((tl + 2 * hl, cw), F32)],
        compiler_params=_cparams(("parallel", "parallel", "parallel")),
        name="dn_conv_qk" if l2 else "dn_conv_v",
    )(proj_l, proj_l, proj_l, proj_c, conv_w)


def _dn_gate_kernel(ab_ref, abc_ref, a_ref, dt_ref, isa_ref, isf_ref, o_ref):
    is_ctx = pl.program_id(1) == pl.num_programs(1) - 1
    x = jnp.where(is_ctx, abc_ref[0], ab_ref[0])
    c = x.shape[0]
    z = x + dt_ref[...]
    softplus = jnp.maximum(z, 0.0) + jnp.log(1.0 + jnp.exp(-jnp.abs(z)))
    g = -a_ref[...] * softplus
    beta = jax.nn.sigmoid(x)
    ri = lax.broadcasted_iota(jnp.int32, (c, c), 0)
    ci = lax.broadcasted_iota(jnp.int32, (c, c), 1)
    tri_f = jnp.where(ri >= ci, 1.0, 0.0).astype(BF16)
    tri_b = jnp.where(ri <= ci, 1.0, 0.0).astype(BF16)
    g1 = g.astype(BF16)
    r1 = g - g1.astype(F32)
    g2 = r1.astype(BF16)
    g3 = (r1 - g2.astype(F32)).astype(BF16)
    cum_f = _dot(tri_f, g1) + _dot(tri_f, g2) + _dot(tri_f, g3)
    cum_b = _dot(tri_b, g1) + _dot(tri_b, g2) + _dot(tri_b, g3)
    gc = jnp.where(isf_ref[...] > 0.5, cum_f, cum_b)
    o_ref[0] = jnp.where(isa_ref[...] > 0.5, gc, beta)


def _dn_gates(ab_l, ab_c, a_lane, dt_lane, isa_lane, isf_lane):
    b, l, n = ab_l.shape
    c = DN_CHUNK
    assert ab_c.shape[1] == c and l % c == 0
    n_lt = l // c
    vec = pl.BlockSpec((1, n), lambda bi, i: (0, 0))
    return pl.pallas_call(
        _dn_gate_kernel,
        out_shape=jax.ShapeDtypeStruct((b, l + c, n), F32),
        grid=(b, n_lt + 1),
        in_specs=[pl.BlockSpec((1, c, n), lambda bi, i: (bi, jnp.minimum(i, n_lt - 1), 0)),
                  pl.BlockSpec((1, c, n), lambda bi, i: (bi, 0, 0)), vec, vec, vec, vec],
        out_specs=pl.BlockSpec((1, c, n), lambda bi, i: (bi, i, 0)),
        compiler_params=_cparams(("parallel", "parallel")),
        name="dn_gates",
    )(ab_l, ab_c, a_lane, dt_lane, isa_lane, isf_lane)


def _mm1(a, b):
    return _dot(a.astype(BF16), b.astype(BF16))


def _tri_inverse(ms, in_block, mm):
    c = ms[0].shape[0]
    ri = lax.broadcasted_iota(jnp.int32, (c, c), 0)
    ci = lax.broadcasted_iota(jnp.int32, (c, c), 1)
    eye = jnp.where(ri == ci, 1.0, 0.0).astype(F32)
    mds = [jnp.where(in_block, m, 0.0) for m in ms]
    rests = [m - md for m, md in zip(ms, mds)]
    ts = [eye - md for md in mds]
    ps = [mm(md, md) for md in mds]
    s = 2
    while s < INV_BASE:
        ts = [t + mm(p, t) for p, t in zip(ps, ts)]
        s *= 2
        if s < INV_BASE:
            ps = [mm(p, p) for p in ps]
    ns = [mm(t, rest) for t, rest in zip(ts, rests)]
    rs = [eye - n for n in ns]
    ps = [mm(n, n) for n in ns]
    s = 2
    while s < c // INV_BASE:
        rs = [r + mm(p, r) for p, r in zip(ps, rs)]
        s *= 2
        if s < c // INV_BASE:
            ps = [mm(p, p) for p in ps]
    return [mm(r, t) for r, t in zip(rs, ts)]


def _delta_step(dirs, grow_ref, s_ref):
    c = dirs[0][0].shape[0]
    dh = DN_HEAD
    ri = lax.broadcasted_iota(jnp.int32, (c, c), 0)
    ci = lax.broadcasted_iota(jnp.int32, (c, c), 1)
    sh = jnp.int32(int(math.log2(INV_BASE)))
    in_block = lax.shift_right_logical(ri, sh) == lax.shift_right_logical(ci, sh)
    streams = []
    grams = []
    for d, (q, k, v2, gcol, r0, lane_g0, rev) in enumerate(dirs):
        strict, incl = (ri < ci, ri <= ci) if rev else (ri > ci, ri >= ci)
        lane = lax.broadcasted_iota(jnp.int32, gcol.shape, 1)
        kk = _dot_nt(k, k)
        qk = _dot_nt(q, k)
        grams.append(qk)
        for s in range(2):
            lg = lane_g0 + s
            lb = lg + gcol.shape[1] // 4
            gc_col = jnp.sum(jnp.where(lane == lg, gcol, 0.0), axis=-1, keepdims=True)
            b_col = jnp.sum(jnp.where(lane == lb, gcol, 0.0), axis=-1, keepdims=True)
            gc_row = grow_ref[0, pl.ds(lg, 1), pl.ds(r0, c)]
            decay = jnp.where(incl, jnp.exp(jnp.minimum(gc_col - gc_row, 0.0)), 0.0)
            m = jnp.where(strict, b_col * kk * decay, 0.0)
            streams.append((d, s, gc_col, b_col, decay, m))
    ainvs = _tri_inverse([st[5] for st in streams], in_block, _mm1)

    kqs = []
    for d, (q, k, *_rest) in enumerate(dirs):
        s_cat = jnp.concatenate([s_ref[2 * d], s_ref[2 * d + 1]], axis=1).astype(BF16)
        kqs.append(_dot(jnp.concatenate([k, q], axis=0), s_cat))
    v_news = []
    for (d, s, gc_col, b_col, decay, m), ainv in zip(streams, ainvs):
        v = dirs[d][2][:, s * dh:(s + 1) * dh].astype(F32)
        ks = kqs[d][:c, s * dh:(s + 1) * dh]
        rhs = (b_col * (v - jnp.exp(gc_col) * ks)).astype(BF16)
        ah, al = _split(ainv)
        v_news.append((_dot(ah, rhs) + _dot(al, rhs)).astype(BF16))
    outs = [[None, None] for _ in dirs]
    for (d, s, gc_col, b_col, decay, m), vb in zip(streams, v_news):
        k, rev = dirs[d][1], dirs[d][6]
        qs = kqs[d][c:, s * dh:(s + 1) * dh]
        outs[d][s] = jnp.exp(gc_col) * qs + _dot((grams[d] * decay).astype(BF16), vb)
        last = 0 if rev else c - 1
        gl = gc_col[last:last + 1, :]
        kd = jnp.exp(gl - gc_col) * k.astype(F32)
        s_ref[2 * d + s] = jnp.exp(gl) * s_ref[2 * d + s] + _dot(kd.T.astype(BF16), vb)
    return [jnp.concatenate(o, axis=1) for o in outs]


def _delta_kernel(q_ref, k_ref, v_ref, gcol_ref, grow_ref, o_ref, s_ref, *, n_vh):
    h = pl.program_id(1)
    c = DN_CHUNK
    n_chunks = q_ref.shape[1] // c
    n_lat = n_chunks - 1
    s_ref[...] = jnp.zeros_like(s_ref)
    o_ref[...] = jnp.zeros_like(o_ref)

    def body(i, carry):
        dirs, rows = [], []
        for rev in (False, True):
            lat_ch = (n_lat - i) if rev else (i - 1)
            r0 = pl.multiple_of(jnp.where(i == 0, n_lat, lat_ch) * c, c)
            rows.append(pl.ds(r0, c))
            dirs.append((q_ref[0, rows[-1], :], k_ref[0, rows[-1], :], v_ref[0, rows[-1], :],
                         gcol_ref[0, rows[-1], :], r0, (2 * n_vh if rev else 0) + 2 * h, rev))
        for r, o in zip(rows, _delta_step(dirs, grow_ref, s_ref)):
            o_ref[0, r, :] = o_ref[0, r, :] + o
        return carry

    lax.fori_loop(0, n_chunks, body, 0)


def _delta_rule(qk, v, gcol, grow):
    b, t, qk2 = qk.shape
    dh = DN_HEAD
    n_qh = qk2 // 2 // dh
    n_vh = v.shape[2] // dh
    ng = gcol.shape[2]
    assert t % DN_CHUNK == 0 and n_vh == 2 * n_qh and ng == 4 * n_vh
    kern = functools.partial(_delta_kernel, n_vh=n_vh)
    return pl.pallas_call(
        kern,
        out_shape=jax.ShapeDtypeStruct((b, t, n_vh * dh), F32),
        grid=(b, n_qh),
        in_specs=[
            pl.BlockSpec((1, t, dh), lambda bi, h: (bi, 0, h)),
            pl.BlockSpec((1, t, dh), lambda bi, h: (bi, 0, h + n_qh)),
            pl.BlockSpec((1, t, 2 * dh), lambda bi, h: (bi, 0, h)),
            pl.BlockSpec((1, t, ng), lambda bi, h: (bi, 0, 0)),
            pl.BlockSpec((1, ng, t), lambda bi, h: (bi, 0, 0)),
        ],
        out_specs=pl.BlockSpec((1, t, 2 * dh), lambda bi, h: (bi, 0, h)),
        scratch_shapes=[pltpu.VMEM((4, dh, dh), F32)],
        compiler_params=_cparams(("parallel", "parallel")),
        name="delta_rule",
    )(qk, qk, v, gcol, grow)


def _dn_out_kernel(o_in_ref, z_ref, og_ref, w_ref, x_ref, gt_ref, o_ref, h_scr):
    @pl.when(pl.program_id(2) == 0)
    def _():
        n_h = h_scr.shape[1] // DN_HEAD
        for hh in range(n_h):
            sl = slice(hh * DN_HEAD, (hh + 1) * DN_HEAD)
            o = o_in_ref[0, :, sl]
            ms = jnp.mean(o * o, axis=-1, keepdims=True)
            y = (o * lax.rsqrt(ms + EPS)) * og_ref[...]
            h_scr[:, sl] = (y * _silu(z_ref[0, :, sl].astype(F32))).astype(BF16)

    y = _dot(h_scr[...], w_ref[...])
    o_ref[0] = x_ref[0] + gt_ref[...] * y


def _dn_out_proj(o_all, row_off, proj3, z_off, onorm_g, w_bf, x3, mod3, batch_row, tm):
    b, l, d = x3.shape
    e = o_all.shape[2]
    tn = 512
    assert row_off % tm == 0 and l % tm == 0 and z_off % e == 0 and d % tn == 0
    zoff = z_off // e
    ooff = row_off // tm
    return pl.pallas_call(
        _dn_out_kernel,
        out_shape=jax.ShapeDtypeStruct((b, l, d), F32),
        grid=(b, l // tm, d // tn),
        in_specs=[
            pl.BlockSpec((1, tm, e), lambda bi, i, j: (bi, i + ooff, 0)),
            pl.BlockSpec((1, tm, e), lambda bi, i, j: (bi, i, zoff)),
            pl.BlockSpec((1, DN_HEAD), lambda bi, i, j: (0, 0)),
            pl.BlockSpec((e, tn), lambda bi, i, j: (0, j)),
            pl.BlockSpec((1, tm, tn), lambda bi, i, j: (bi, i, j)),
            pl.BlockSpec((None, 1, tn), lambda bi, i, j: (batch_row(bi), 0, 2 * (d // tn) + j)),
        ],
        out_specs=pl.BlockSpec((1, tm, tn), lambda bi, i, j: (bi, i, j)),
        scratch_shapes=[pltpu.VMEM((tm, e), BF16)],
        compiler_params=_cparams(("parallel", "parallel", "arbitrary")),
        name="dn_out_proj",
    )(o_all, proj3, onorm_g.reshape(1, DN_HEAD), w_bf, x3, mod3)


def _cv_in_kernel(x_ref, sc_ref, sh_ref, g_ref, wa_ref, wg_ref, ba_ref, bg_ref, o_ref, h_scr):
    @pl.when(pl.program_id(1) == 0)
    def _():
        h_scr[...] = _norm_mod(x_ref[...], g_ref[...], sc_ref[...], sh_ref[...]).astype(BF16)

    h = h_scr[...]
    a = _dot(h, wa_ref[...]) + ba_ref[...]
    gate = _dot(h, wg_ref[...]) + bg_ref[...]
    o_ref[...] = (a * jax.nn.sigmoid(gate)).astype(o_ref.dtype)


def _cv_in_proj(x2, mod3, g, w1_bf, b1, row_fn, tm):
    m, d = x2.shape
    ci = w1_bf.shape[1] // 2
    tn = 1024
    nj = ci // tn
    b1r = b1.reshape(1, 2 * ci)
    return pl.pallas_call(
        _cv_in_kernel,
        out_shape=jax.ShapeDtypeStruct((m, ci), BF16),
        grid=(m // tm, nj),
        in_specs=[
            pl.BlockSpec((tm, d), lambda i, j: (i, 0)),
            _mod_spec(d, 1, row_fn),
            _mod_spec(d, 0, row_fn),
            pl.BlockSpec((1, d), lambda i, j: (0, 0)),
            pl.BlockSpec((d, tn), lambda i, j: (0, j)),
            pl.BlockSpec((d, tn), lambda i, j: (0, j + nj)),
            pl.BlockSpec((1, tn), lambda i, j: (0, j)),
            pl.BlockSpec((1, tn), lambda i, j: (0, j + nj)),
        ],
        out_specs=pl.BlockSpec((tm, tn), lambda i, j: (i, j)),
        scratch_shapes=[pltpu.VMEM((tm, d), BF16)],
        compiler_params=_cparams(("parallel", "arbitrary")),
        name="cv_in_proj",
    )(x2, mod3, mod3, g.reshape(1, d), w1_bf, w1_bf, b1r, b1r)


def _cv_out_kernel(u_ref, up_ref, un_ref, dw_ref, dwb_ref, lg_ref, lb_ref, w_ref, b2_ref, x_ref, gt_ref,
                   o_ref, ext_scr, sh_scr, conv_scr, h_scr, *, tm, taps, rb, cch):
    i = pl.program_id(1)
    n_i = pl.num_programs(1)
    hl = CONV_HALO
    sub = 8
    span = sh_scr.shape[1]

    @pl.when(pl.program_id(2) == 0)
    def _():
        prev = up_ref[0].astype(F32)
        nxt = un_ref[0].astype(F32)
        ext_scr[0:hl, :] = jnp.where(i > 0, prev, 0.0)
        ext_scr[hl:hl + tm, :] = u_ref[0].astype(F32)
        ext_scr[hl + tm:, :] = jnp.where(i < n_i - 1, nxt, 0.0)

        for cc in range(ext_scr.shape[1] // cch):
            lanes = slice(cc * cch, (cc + 1) * cch)
            for r in range(1, sub):
                sh_scr[r - 1] = ext_scr[r:r + span, lanes]

            def rows(rbi, carry):
                r0 = pl.multiple_of(rbi * rb, rb)
                acc = None
                for k in range(taps):
                    off = hl - taps // 2 + k
                    r, a = off % sub, off - off % sub
                    if r == 0:
                        src = ext_scr[pl.ds(r0 + a, rb), lanes]
                    else:
                        src = sh_scr[r - 1, pl.ds(r0 + a, rb), :]
                    term = dw_ref[k:k + 1, lanes] * src
                    acc = term if acc is None else acc + term
                conv_scr[pl.ds(r0, rb), lanes] = acc + dwb_ref[:, lanes]
                return carry

            lax.fori_loop(0, tm // rb, rows, 0)

        def norm_rows(rbi, carry):
            r0 = pl.multiple_of(rbi * rb, rb)
            acc = conv_scr[pl.ds(r0, rb), :]
            mu = jnp.mean(acc, axis=-1, keepdims=True)
            xc = acc - mu
            var = jnp.mean(xc * xc, axis=-1, keepdims=True)
            y = (xc * lax.rsqrt(var + EPS)) * lg_ref[...] + lb_ref[...]
            h_scr[pl.ds(r0, rb), :] = _silu(y).astype(BF16)
            return carry

        lax.fori_loop(0, tm // rb, norm_rows, 0)

    y = _dot(h_scr[...], w_ref[...]) + b2_ref[...]
    o_ref[0] = x_ref[0] + gt_ref[...] * y


def _cv_out_proj(u, dw, dwb, ln_g, ln_b, w2_bf, b2, x3, mod3, batch_row, tm):
    b, l, ci = u.shape
    d = w2_bf.shape[1]
    taps = dw.shape[0]
    tn = 1024
    hl = CONV_HALO
    rpb = tm // hl
    last = l // hl - 1
    cch = min(ci, 512)
    kern = functools.partial(_cv_out_kernel, tm=tm, taps=taps, rb=32, cch=cch)
    vec = lambda n: pl.BlockSpec((1, n), lambda bi, i, j: (0, 0))
    return pl.pallas_call(
        kern,
        out_shape=jax.ShapeDtypeStruct((b, l, d), F32),
        grid=(b, l // tm, d // tn),
        in_specs=[
            pl.BlockSpec((1, tm, ci), lambda bi, i, j: (bi, i, 0)),
            pl.BlockSpec((1, hl, ci), lambda bi, i, j: (bi, jnp.maximum(i * rpb - 1, 0), 0)),
            pl.BlockSpec((1, hl, ci), lambda bi, i, j: (bi, jnp.minimum((i + 1) * rpb, last), 0)),
            pl.BlockSpec((taps, ci), lambda bi, i, j: (0, 0)),
            vec(ci), vec(ci), vec(ci),
            pl.BlockSpec((ci, tn), lambda bi, i, j: (0, j)),
            pl.BlockSpec((1, tn), lambda bi, i, j: (0, j)),
            pl.BlockSpec((1, tm, tn), lambda bi, i, j: (bi, i, j)),
            pl.BlockSpec((None, 1, tn), lambda bi, i, j: (batch_row(bi), 0, 2 * (d // tn) + j)),
        ],
        out_specs=pl.BlockSpec((1, tm, tn), lambda bi, i, j: (bi, i, j)),
        scratch_shapes=[pltpu.VMEM((tm + 2 * hl, ci), F32),
                        pltpu.VMEM((7, tm + 2 * hl - 8, cch), F32),
                        pltpu.VMEM((tm, ci), F32),
                        pltpu.VMEM((tm, ci), BF16)],
        compiler_params=_cparams(("parallel", "parallel", "arbitrary")),
        name="cv_out_proj",
    )(u, u, u, dw, dwb.reshape(1, ci), ln_g.reshape(1, ci), ln_b.reshape(1, ci), w2_bf,
      b2.reshape(1, d), x3, mod3)


def _router_kernel(x_ref, sc_ref, sh_ref, g_ref, w_ref, b_ref, h_ref, eid_ref, wt_ref):
    h = _norm_mod(x_ref[...], g_ref[...], sc_ref[...], sh_ref[...])
    h_ref[...] = _pack_bf16_pairs(h)
    logits = _dot3(h, w_ref[...]) + b_ref[...]
    lane = lax.broadcasted_iota(jnp.int32, logits.shape, 1)
    neg = jnp.float32(-1e30)
    big = jnp.int32(ROUTER_LANES)
    is_grp = lane < N_GROUPS
    gl = jnp.where(is_grp, logits, neg)
    gmax = jnp.max(gl, axis=-1, keepdims=True)
    gsum = jnp.sum(jnp.where(is_grp, jnp.exp(gl - gmax), 0.0), axis=-1, keepdims=True)
    grp_p = 1.0 / gsum
    gidx = jnp.min(jnp.where(gl == gmax, lane, big), axis=-1, keepdims=True)
    lo = N_GROUPS + gidx * EXPERTS_PER_GROUP
    in_g = jnp.logical_and(lane >= lo, lane < lo + EXPERTS_PER_GROUP)
    el = jnp.where(in_g, logits, neg)
    emax = jnp.max(el, axis=-1, keepdims=True)
    ex = jnp.where(in_g, jnp.exp(el - emax), 0.0)
    prob = jnp.where(in_g, ex / jnp.sum(ex, axis=-1, keepdims=True), -1.0)
    p1 = jnp.max(prob, axis=-1, keepdims=True)
    i1 = jnp.min(jnp.where(prob == p1, lane, big), axis=-1, keepdims=True)
    prob2 = jnp.where(lane == i1, -1.0, prob)
    p2 = jnp.max(prob2, axis=-1, keepdims=True)
    i2 = jnp.min(jnp.where(prob2 == p2, lane, big), axis=-1, keepdims=True)
    denom = p1 + p2
    w1 = grp_p * p1 / denom
    w2 = grp_p * p2 / denom
    eid_ref[...] = jnp.where(lane == 0, i1 - N_GROUPS, jnp.where(lane == 1, i2 - N_GROUPS, 0))
    wt_ref[...] = jnp.where(lane == 0, w1, jnp.where(lane == 1, w2, 0.0))


def _router(x2, mod3, g, w_r, b_r, row_fn, tm):
    m, d = x2.shape
    n = ROUTER_LANES
    return pl.pallas_call(
        _router_kernel,
        out_shape=(jax.ShapeDtypeStruct((m, d // 2), jnp.uint32), jax.ShapeDtypeStruct((m, n), jnp.int32),
                   jax.ShapeDtypeStruct((m, n), F32)),
        grid=(m // tm, 1),
        in_specs=[
            pl.BlockSpec((tm, d), lambda i, j: (i, 0)),
            _mod_spec(d, 4, row_fn),
            _mod_spec(d, 3, row_fn),
            pl.BlockSpec((1, d), lambda i, j: (0, 0)),
            pl.BlockSpec((d, n), lambda i, j: (0, 0)),
            pl.BlockSpec((1, n), lambda i, j: (0, 0)),
        ],
        out_specs=(pl.BlockSpec((tm, d // 2), lambda i, j: (i, 0)),
                   pl.BlockSpec((tm, n), lambda i, j: (i, 0)),
                   pl.BlockSpec((tm, n), lambda i, j: (i, 0))),
        compiler_params=_cparams(("parallel", "arbitrary")),
        name="moe_router",
    )(x2, mod3, mod3, g.reshape(1, d), w_r, b_r)


def _row_gather_start(idx_ref, n_rows, src_hbm, dst, sem):
    def issue(r, carry):
        pltpu.make_async_copy(src_hbm.at[pl.ds(idx_ref[0, 0, r], 1), :], dst.at[pl.ds(r, 1), :], sem).start()
        return carry

    lax.fori_loop(0, n_rows, issue, 0, unroll=8)


def _row_gather_wait(dst, sem):
    pltpu.make_async_copy(dst, dst, sem).wait()


def _prefetched_gather(idx_ref, idx_next_ref, n_rows, src_hbm, buf, sem):
    i = pl.program_id(0)
    slot = lax.rem(i, 2)

    @pl.when(i == 0)
    def _():
        _row_gather_start(idx_ref, n_rows, src_hbm, buf.at[0], sem.at[0])

    @pl.when(i + 1 < pl.num_programs(0))
    def _():
        _row_gather_start(idx_next_ref, n_rows, src_hbm, buf.at[1 - slot], sem.at[1 - slot])

    _row_gather_wait(buf.at[slot], sem.at[slot])
    return slot


def _expert_kernel(be_ref, nu_ref, idx_ref, idx_next_ref, h_hbm, wgu_ref, wd_ref, o_ref,
                   xbuf, sem, wgu_bf, wd_bf):
    i = pl.program_id(0)
    slot = _prefetched_gather(idx_ref, idx_next_ref, MOE_BLOCK, h_hbm, xbuf, sem)
    prev = be_ref[jnp.maximum(i - 1, 0)]
    changed = jnp.logical_or(i == 0, be_ref[i] != prev)

    @pl.when(changed)
    def _():
        wgu_bf[...] = wgu_ref[...].astype(BF16)
        wd_bf[...] = wd_ref[...].astype(BF16)

    @pl.when(i < nu_ref[0])
    def _():
        de = wd_bf.shape[0]
        x = _unpack_bf16_pairs(xbuf[slot]).astype(BF16)
        gu = _dot(x, wgu_bf[...])
        hmid = (_silu(gu[:, :de]) * gu[:, de:]).astype(BF16)
        o_ref[...] = _pack_bf16_pairs(_dot(hmid, wd_bf[...]))

    @pl.when(i >= nu_ref[0])
    def _():
        o_ref[...] = jnp.zeros_like(o_ref)


def _experts(block_expert, n_used, slot_tok, h_packed, w_gu, w_down, layer):
    n_blocks = slot_tok.shape[0]
    dp = h_packed.shape[1]
    d = 2 * dp
    de2 = w_gu.shape[3]
    de = de2 // 2
    grid_spec = pltpu.PrefetchScalarGridSpec(
        num_scalar_prefetch=2,
        grid=(n_blocks,),
        in_specs=[
            pl.BlockSpec((1, 1, MOE_BLOCK), lambda i, be, nu: (i, 0, 0), memory_space=pltpu.SMEM),
            pl.BlockSpec((1, 1, MOE_BLOCK), lambda i, be, nu: (jnp.minimum(i + 1, n_blocks - 1), 0, 0),
                         memory_space=pltpu.SMEM),
            pl.BlockSpec(memory_space=pl.ANY),
            pl.BlockSpec((None, None, d, de2), lambda i, be, nu: (layer, be[i], 0, 0)),
            pl.BlockSpec((None, None, de, d), lambda i, be, nu: (layer, be[i], 0, 0)),
        ],
        out_specs=pl.BlockSpec((MOE_BLOCK, dp), lambda i, be, nu: (i, 0)),
        scratch_shapes=[pltpu.VMEM((2, MOE_BLOCK, dp), jnp.uint32), pltpu.SemaphoreType.DMA((2,)),
                        pltpu.VMEM((d, de2), BF16), pltpu.VMEM((de, d), BF16)],
    )
    return pl.pallas_call(
        _expert_kernel,
        out_shape=jax.ShapeDtypeStruct((n_blocks * MOE_BLOCK, dp), jnp.uint32),
        grid_spec=grid_spec,
        compiler_params=_cparams(("arbitrary",)),
        name="moe_experts",
    )(block_expert, n_used, slot_tok, slot_tok, h_packed, w_gu, w_down)


def _combine_kernel(idx_ref, idx_next_ref, ys_hbm, x_ref, w_ref, gt_ref, o_ref, ybuf, sem, *, tm):
    slot = _prefetched_gather(idx_ref, idx_next_ref, 2 * tm, ys_hbm, ybuf, sem)
    w = w_ref[...]
    y0 = _unpack_bf16_pairs(ybuf[slot, 0:tm, :])
    y1 = _unpack_bf16_pairs(ybuf[slot, tm:2 * tm, :])
    o_ref[...] = x_ref[...] + gt_ref[...] * (w[:, 0:1] * y0 + w[:, 1:2] * y1)


def _combine(x2, ys_packed, slot_of_assign, wts, mod3, row_fn, tm):
    m, d = x2.shape
    dp = ys_packed.shape[1]
    assert m % tm == 0 and d == 2 * dp
    n_t = m // tm
    idx = slot_of_assign.reshape(n_t, tm, 2).transpose(0, 2, 1).reshape(n_t, 1, 2 * tm)
    kern = functools.partial(_combine_kernel, tm=tm)
    return pl.pallas_call(
        kern,
        out_shape=jax.ShapeDtypeStruct((m, d), F32),
        grid=(n_t, 1),
        in_specs=[
            pl.BlockSpec((1, 1, 2 * tm), lambda i, j: (i, 0, 0), memory_space=pltpu.SMEM),
            pl.BlockSpec((1, 1, 2 * tm), lambda i, j: (jnp.minimum(i + 1, n_t - 1), 0, 0),
                         memory_space=pltpu.SMEM),
            pl.BlockSpec(memory_space=pl.ANY),
            pl.BlockSpec((tm, d), lambda i, j: (i, 0)),
            pl.BlockSpec((tm, ROUTER_LANES), lambda i, j: (i, 0)),
            _mod_spec(d, 5, row_fn),
        ],
        out_specs=pl.BlockSpec((tm, d), lambda i, j: (i, 0)),
        scratch_shapes=[pltpu.VMEM((2, 2 * tm, dp), jnp.uint32), pltpu.SemaphoreType.DMA((2,))],
        compiler_params=_cparams(("arbitrary", "arbitrary")),
        name="moe_combine",
    )(idx, idx, ys_packed, x2, wts, mod3)


def _final_kernel(x_ref, g_ref, o_ref):
    x = x_ref[...]
    ms = jnp.mean(x * x, axis=-1, keepdims=True)
    o_ref[...] = (x * lax.rsqrt(ms + EPS)) * g_ref[...]


def _final_norm(x2, g, tm):
    m, d = x2.shape
    return pl.pallas_call(
        _final_kernel,
        out_shape=jax.ShapeDtypeStruct((m, d), F32),
        grid=(m // tm,),
        in_specs=[pl.BlockSpec((tm, d), lambda i: (i, 0)), pl.BlockSpec((1, d), lambda i: (0, 0))],
        out_specs=pl.BlockSpec((tm, d), lambda i: (i, 0)),
        compiler_params=_cparams(("parallel",)),
        name="final_norm",
    )(x2, g.reshape(1, d))


def _to_column_major(h):
    b, n, d = h.shape
    rows = n // GRID_W
    return h.reshape(b, rows, GRID_W, d).transpose(0, 2, 1, 3).reshape(b, n, d)


def _from_column_major(h):
    b, n, d = h.shape
    rows = n // GRID_W
    return h.reshape(b, GRID_W, rows, d).transpose(0, 2, 1, 3).reshape(b, n, d)


def _dispatch_tables(eid):
    n_t, top_k = eid.shape
    n_assign = n_t * top_k
    i32 = jnp.int32
    e_flat = eid.reshape(-1)
    order = jnp.argsort(e_flat).astype(i32)
    rank = jnp.argsort(order).astype(i32)
    onehot = e_flat[:, None] == jnp.arange(N_EXPERTS, dtype=i32)[None, :]
    counts = jnp.sum(onehot.astype(i32), axis=0)
    padded = (counts + MOE_BLOCK - 1) // MOE_BLOCK * MOE_BLOCK
    starts = jnp.cumsum(counts) - counts
    pad_ends = jnp.cumsum(padded)
    pad_starts = pad_ends - padded
    lookup = lambda table: jnp.sum(jnp.where(onehot, table[None, :], 0), axis=1)
    slot_of_assign = lookup(pad_starts - starts) + rank
    n_blocks = -(-n_assign // MOE_BLOCK) + N_EXPERTS
    n_used = (pad_ends[-1] // MOE_BLOCK).astype(i32)
    blk = jnp.arange(n_blocks, dtype=i32)
    blk_c = jnp.minimum(blk, n_used - 1)
    block_expert = jnp.minimum(
        jnp.sum((pad_ends[None, :] <= (blk_c * MOE_BLOCK)[:, None]).astype(i32), axis=1), N_EXPERTS - 1)
    be_hot = block_expert[:, None] == jnp.arange(N_EXPERTS, dtype=i32)[None, :]
    blk_lookup = lambda table: jnp.sum(jnp.where(be_hot, table[None, :], 0), axis=1)
    row0 = blk_c * MOE_BLOCK - blk_lookup(pad_starts)
    n_valid = jnp.where(blk < n_used, blk_lookup(counts) - row0, 0)
    r = jnp.arange(MOE_BLOCK, dtype=i32)[None, :]
    pos = (blk_lookup(starts) + row0)[:, None] + r
    valid = r < n_valid[:, None]
    src = jnp.take(order, jnp.clip(pos, 0, n_assign - 1).reshape(-1), mode="clip")
    slot_tok = jnp.where(valid, src.reshape(n_blocks, MOE_BLOCK) // top_k, 0)
    return (slot_tok.reshape(n_blocks, 1, MOE_BLOCK), slot_of_assign.reshape(n_t, top_k), block_expert,
            n_used.reshape(1))


def _moe(h_packed, eid, w_gu, w_down, layer):
    slot_tok, slot_of_assign, block_expert, n_used = _dispatch_tables(eid)
    ys = _experts(block_expert, n_used, slot_tok, h_packed, w_gu, w_down, layer)
    return ys, slot_of_assign


def _gate_lane_params(a_log, dt_bias):
    n_dir, n_h = a_log.shape
    a = jnp.exp(a_log.astype(F32))
    zeros = jnp.zeros_like(a)
    ones = jnp.ones_like(a)
    lane = lambda av, bv: jnp.stack([av, bv], axis=1).reshape(1, n_dir * 2 * n_h)
    isf = jnp.broadcast_to((jnp.arange(n_dir) == 0).astype(F32)[:, None], a.shape)
    return lane(a, zeros), lane(dt_bias.astype(F32), zeros), lane(ones, zeros), lane(isf, isf)


def kernel(x, c, ctx, c_ctx, ada_w, ada_b, norm1_g, norm2_g, dn_w_in, dn_conv_w, dn_a_log, dn_dt_bias, dn_onorm_g, dn_w_out, cv_w1, cv_b1, cv_dw, cv_dwb, cv_ln_g, cv_ln_b, cv_w2, cv_b2, moe_w_grp, moe_b_grp, moe_w_exp, moe_b_exp, moe_w_gu, moe_w_down, final_g):
    bsz, n_lat, d = x.shape
    lc = ctx.shape[1]
    depth = ada_w.shape[0]
    conv_dim = dn_conv_w.shape[2]
    v_dim = dn_w_out.shape[1]
    qk_dim = (conv_dim - v_dim) // 2
    ml, mc = bsz * n_lat, bsz * lc
    tm_l = min(n_lat, 1024)
    tm_c = min(mc, 1024)
    tm_s = min(n_lat, mc, 512)
    ctx_row = bsz
    lat_row = lambda i: (i * tm_l) // n_lat
    lat_row_s = lambda i: (i * tm_s) // n_lat
    tm_g = min(n_lat, mc, 256)
    lat_row_g = lambda i: (i * tm_g) // n_lat
    ctx_rowf = lambda i: ctx_row

    cvec = jnp.concatenate([c, c_ctx[None, :], jnp.zeros((8 - bsz - 1, d), F32)], axis=0)
    mods = _ada_mod(cvec, ada_w, ada_b)

    xl = x.reshape(ml, d)
    xc = ctx.reshape(mc, d)
    col_major_now = False
    for i in range(depth):
        last = i == depth - 1
        j = i // 2
        mod3 = mods[i].reshape(8, 1, 6 * d)
        col_major = (i // 2) % 2 == 1
        if col_major != col_major_now:
            xl3 = xl.reshape(bsz, n_lat, d)
            xl3 = _to_column_major(xl3) if col_major else _from_column_major(xl3)
            xl = xl3.reshape(ml, d)
            col_major_now = col_major

        if i % 2 == 0:
            w_in = dn_w_in[j]
            w_main = w_in[:, :conv_dim + v_dim].astype(BF16)
            w_ab = w_in[:, conv_dim + v_dim:]
            lanes = _gate_lane_params(dn_a_log[j], dn_dt_bias[j])
            n_ab = w_ab.shape[1]
            n_proj = conv_dim + v_dim
            proj_c, ab_c = _dn_in_proj(xc, mod3, norm1_g[i], w_main, w_ab, ctx_rowf, tm_c)
            proj_l, ab_l = _dn_in_proj(xl, mod3, norm1_g[i], w_main, w_ab, lat_row, tm_l)
            proj_c = proj_c.reshape(bsz, lc, n_proj)
            proj_l = proj_l.reshape(bsz, n_lat, n_proj)
            qk = _dn_conv(proj_l, proj_c, dn_conv_w[j], 0, 2 * qk_dim, True, qk_dim)
            v = _dn_conv(proj_l, proj_c, dn_conv_w[j], 2 * qk_dim, v_dim, False, qk_dim)
            gcol = _dn_gates(ab_l.reshape(bsz, n_lat, n_ab), ab_c.reshape(bsz, lc, n_ab), *lanes)
            o_all = _delta_rule(qk, v, gcol, jnp.swapaxes(gcol, 1, 2))
            w_out = dn_w_out[j].astype(BF16)
            xl = _dn_out_proj(o_all, 0, proj_l, conv_dim, dn_onorm_g[j], w_out, xl.reshape(bsz, n_lat, d),
                              mod3, lambda bi: bi, tm_s).reshape(ml, d)
            xc = _dn_out_proj(o_all, n_lat, proj_c, conv_dim, dn_onorm_g[j], w_out, xc.reshape(bsz, lc, d),
                              mod3, lambda bi: ctx_row, lc).reshape(mc, d)
        else:
            w1 = cv_w1[j].astype(BF16)
            w2 = cv_w2[j].astype(BF16)
            ci = w2.shape[0]
            cv_tail = (cv_dw[j], cv_dwb[j], cv_ln_g[j], cv_ln_b[j], w2, cv_b2[j])
            ul = _cv_in_proj(xl, mod3, norm1_g[i], w1, cv_b1[j], lat_row, tm_l)
            xl = _cv_out_proj(ul.reshape(bsz, n_lat, ci), *cv_tail, xl.reshape(bsz, n_lat, d), mod3,
                              lambda bi: bi, min(n_lat, 512)).reshape(ml, d)
            if not last:
                uc = _cv_in_proj(xc, mod3, norm1_g[i], w1, cv_b1[j], ctx_rowf, tm_c)
                xc = _cv_out_proj(uc.reshape(bsz, lc, ci), *cv_tail, xc.reshape(bsz, lc, d), mod3,
                                  lambda bi: ctx_row, lc).reshape(mc, d)

        w_r = jnp.concatenate([moe_w_grp[i], moe_w_exp[i],
                               jnp.zeros((d, ROUTER_LANES - N_GROUPS - N_EXPERTS), F32)], axis=1)
        b_r = jnp.concatenate([moe_b_grp[i], moe_b_exp[i],
                               jnp.zeros((ROUTER_LANES - N_GROUPS - N_EXPERTS,), F32)])[None, :]
        hl2, eid_l, wt_l = _router(xl, mod3, norm2_g[i], w_r, b_r, lat_row, tm_l)
        if last:
            ys, slots = _moe(hl2, eid_l[:, :2], moe_w_gu, moe_w_down, i)
            xl = _combine(xl, ys, slots, wt_l, mod3, lat_row_g, tm_g)
        else:
            hc2, eid_c, wt_c = _router(xc, mod3, norm2_g[i], w_r, b_r, ctx_rowf, tm_c)
            h_all = jnp.concatenate([hc2, hl2], axis=0)
            eid_all = jnp.concatenate([eid_c[:, :2], eid_l[:, :2]], axis=0)
            ys, slots = _moe(h_all, eid_all, moe_w_gu, moe_w_down, i)
            xc = _combine(xc, ys, slots[:mc], wt_c, mod3, ctx_rowf, tm_g)
            xl = _combine(xl, ys, slots[mc:], wt_l, mod3, lat_row_g, tm_g)

    out = _final_norm(xl, final_g, tm_l).reshape(bsz, n_lat, d)
    if col_major_now:
        out = _from_column_major(out)
    return out
```

```python
import functools
import math

import jax
import jax.numpy as jnp
from jax import lax
from jax.experimental import pallas as pl
from jax.experimental.pallas import tpu as pltpu

F32 = jnp.float32
BF16 = jnp.bfloat16
EPS = 1e-6

GRID_W = 64
DN_HEAD = 128
DN_CHUNK = 256
INV_BASE = 16
N_GROUPS = 8
EXPERTS_PER_GROUP = 8
N_EXPERTS = N_GROUPS * EXPERTS_PER_GROUP
MOE_BLOCK = 256
ROUTER_LANES = 128
CONV_HALO = 16
V7X_VMEM_LIMIT = 56 * 1024 * 1024


def _cparams(sem, vmem=V7X_VMEM_LIMIT):
    return pltpu.CompilerParams(dimension_semantics=sem, vmem_limit_bytes=vmem)


def _dot(a, b):
    return jnp.dot(a, b, preferred_element_type=F32)


def _dot_nt(a, b):
    return lax.dot_general(a, b, (((1,), (1,)), ((), ())), preferred_element_type=F32)


def _split(x):
    hi = x.astype(BF16)
    lo = (x - hi.astype(F32)).astype(BF16)
    return hi, lo


def _dot3(a, b):
    ah, al = _split(a)
    bh, bl = _split(b)
    return _dot(ah, bh) + _dot(al, bh) + _dot(ah, bl)


def _silu(x):
    return x * jax.nn.sigmoid(x)


_HI16 = 0xFFFF0000


def _pack_bf16_pairs(x):
    half = x.shape[1] // 2
    bits = lax.bitcast_convert_type(x.astype(BF16).astype(F32), jnp.uint32)
    return (bits[:, :half] >> 16) | (bits[:, half:] & jnp.uint32(_HI16))


def _unpack_bf16_pairs(w):
    lo = lax.bitcast_convert_type(w << 16, F32)
    hi = lax.bitcast_convert_type(w & jnp.uint32(_HI16), F32)
    return jnp.concatenate([lo, hi], axis=1)


def _norm_mod(x, g, sc, sh):
    ms = jnp.mean(x * x, axis=-1, keepdims=True)
    return (x * lax.rsqrt(ms + EPS)) * g * (1.0 + sc) + sh


def _ada_kernel(c_ref, w_ref, b_ref, o_ref):
    s = _silu(c_ref[...])
    o_ref[...] = _dot3(s, w_ref[...]) + b_ref[...]


def _ada_mod(cvec, ada_w, ada_b):
    depth, d, n = ada_w.shape
    tn = 1024
    return pl.pallas_call(
        _ada_kernel,
        out_shape=jax.ShapeDtypeStruct((depth, 8, n), F32),
        grid=(depth, n // tn),
        in_specs=[
            pl.BlockSpec((8, d), lambda l, j: (0, 0)),
            pl.BlockSpec((None, d, tn), lambda l, j: (l, 0, j)),
            pl.BlockSpec((None, 1, tn), lambda l, j: (l, 0, j)),
        ],
        out_specs=pl.BlockSpec((None, 8, tn), lambda l, j: (l, 0, j)),
        compiler_params=_cparams(("parallel", "parallel")),
        name="ada_mod",
    )(cvec, ada_w, ada_b.reshape(depth, 1, n))


def _mod_spec(d, chunk, row_fn):
    return pl.BlockSpec((None, 1, d), lambda i, j: (row_fn(i), 0, chunk))


def _dn_in_kernel(x_ref, sc_ref, sh_ref, g_ref, w_ref, wab_ref, o_ref, ab_ref, h_scr):
    @pl.when(pl.program_id(1) == 0)
    def _():
        h = _norm_mod(x_ref[...], g_ref[...], sc_ref[...], sh_ref[...])
        h_scr[...] = h.astype(BF16)
        ab_ref[...] = _dot3(h, wab_ref[...])

    o_ref[...] = _dot(h_scr[...], w_ref[...]).astype(o_ref.dtype)


def _dn_in_proj(x2, mod3, g, w_bf, wab, row_fn, tm):
    m, d = x2.shape
    n = w_bf.shape[1]
    nab = wab.shape[1]
    tn = 1024
    return pl.pallas_call(
        _dn_in_kernel,
        out_shape=(jax.ShapeDtypeStruct((m, n), BF16), jax.ShapeDtypeStruct((m, nab), F32)),
        grid=(m // tm, n // tn),
        in_specs=[
            pl.BlockSpec((tm, d), lambda i, j: (i, 0)),
            _mod_spec(d, 1, row_fn),
            _mod_spec(d, 0, row_fn),
            pl.BlockSpec((1, d), lambda i, j: (0, 0)),
            pl.BlockSpec((d, tn), lambda i, j: (0, j)),
            pl.BlockSpec((d, nab), lambda i, j: (0, 0)),
        ],
        out_specs=(pl.BlockSpec((tm, tn), lambda i, j: (i, j)),
                   pl.BlockSpec((tm, nab), lambda i, j: (i, 0))),
        scratch_shapes=[pltpu.VMEM((tm, d), BF16)],
        compiler_params=_cparams(("parallel", "arbitrary")),
        name="dn_in_proj",
    )(x2, mod3, mod3, g.reshape(1, d), w_bf, wab)


def _dn_conv_kernel(cur_ref, prev_ref, next_ref, ctx_ref, w_ref, o_ref, ext_scr, *, tl, taps, l2, n_q_heads):
    c = pl.program_id(1)
    i = pl.program_id(2)
    n_lat_tiles = pl.num_programs(2) - 1
    is_ctx = i == n_lat_tiles
    hl = CONV_HALO
    prev = prev_ref[0].astype(F32)
    nxt = next_ref[0].astype(F32)
    ext_scr[0:hl, :] = jnp.where(jnp.logical_and(i > 0, jnp.logical_not(is_ctx)), prev, 0.0)
    ext_scr[hl:hl + tl, :] = jnp.where(is_ctx, ctx_ref[0], cur_ref[0]).astype(F32)
    ext_scr[hl + tl:, :] = jnp.where(i < n_lat_tiles - 1, nxt, 0.0)
    acc = None
    for k in range(taps):
        term = w_ref[k:k + 1, :] * ext_scr[pl.ds(hl - taps // 2 + k, tl), :]
        acc = term if acc is None else acc + term
    y = _silu(acc)
    cw = y.shape[1]
    if l2:
        for hh in range(cw // DN_HEAD):
            is_q = c * (cw // DN_HEAD) + hh < n_q_heads
            scale = jnp.where(is_q, DN_HEAD ** -0.5, 1.0).astype(F32)
            ys = y[:, hh * DN_HEAD:(hh + 1) * DN_HEAD]
            ss = jnp.sum(ys * ys, axis=-1, keepdims=True)
            o_ref[0, :, hh * DN_HEAD:(hh + 1) * DN_HEAD] = (
                ys * (lax.rsqrt(ss + EPS) * scale)).astype(o_ref.dtype)
    else:
        o_ref[0] = y.astype(o_ref.dtype)


def _dn_conv(proj_l, proj_c, conv_w, ch_off, n_ch, l2, qk_dim):
    b, l, _ = proj_l.shape
    lc = proj_c.shape[1]
    taps = conv_w.shape[0]
    tl = lc
    cw = 1024
    assert l % tl == 0 and tl % CONV_HALO == 0 and n_ch % cw == 0 and ch_off % cw == 0
    hl = CONV_HALO
    coff = ch_off // cw
    kern = functools.partial(_dn_conv_kernel, tl=tl, taps=taps, l2=l2, n_q_heads=qk_dim // DN_HEAD)
    rpb = tl // hl
    n_lt = l // tl
    last = l // hl - 1
    return pl.pallas_call(
        kern,
        out_shape=jax.ShapeDtypeStruct((b, l + lc, n_ch), BF16),
        grid=(b, n_ch // cw, n_lt + 1),
        in_specs=[
            pl.BlockSpec((1, tl, cw), lambda bi, c, i: (bi, jnp.minimum(i, n_lt - 1), c + coff)),
            pl.BlockSpec((1, hl, cw), lambda bi, c, i: (bi, jnp.clip(i * rpb - 1, 0, last), c + coff)),
            pl.BlockSpec((1, hl, cw), lambda bi, c, i: (bi, jnp.minimum((i + 1) * rpb, last), c + coff)),
            pl.BlockSpec((1, tl, cw), lambda bi, c, i: (bi, 0, c + coff)),
            pl.BlockSpec((taps, cw), lambda bi, c, i: (0, c + coff)),
        ],
        out_specs=pl.BlockSpec((1, tl, cw), lambda bi, c, i: (bi, i, c)),
        scratch_shapes=[pltpu.VMEM((tl + 2 * hl, cw), F32)],
        compiler_params=_cparams(("parallel", "parallel", "parallel")),
        name="dn_conv_qk" if l2 else "dn_conv_v",
    )(proj_l, proj_l, proj_l, proj_c, conv_w)


def _dn_gate_kernel(ab_ref, abc_ref, a_ref, dt_ref, isa_ref, isf_ref, o_ref):
    is_ctx = pl.program_id(1) == pl.num_programs(1) - 1
    x = jnp.where(is_ctx, abc_ref[0], ab_ref[0])
    c = x.shape[0]
    z = x + dt_ref[...]
    softplus = jnp.maximum(z, 0.0) + jnp.log(1.0 + jnp.exp(-jnp.abs(z)))
    g = -a_ref[...] * softplus
    beta = jax.nn.sigmoid(x)
    ri = lax.broadcasted_iota(jnp.int32, (c, c), 0)
    ci = lax.broadcasted_iota(jnp.int32, (c, c), 1)
    tri_f = jnp.where(ri >= ci, 1.0, 0.0).astype(BF16)
    tri_b = jnp.where(ri <= ci, 1.0, 0.0).astype(BF16)
    g1 = g.astype(BF16)
    r1 = g - g1.astype(F32)
    g2 = r1.astype(BF16)
    g3 = (r1 - g2.astype(F32)).astype(BF16)
    cum_f = _dot(tri_f, g1) + _dot(tri_f, g2) + _dot(tri_f, g3)
    cum_b = _dot(tri_b, g1) + _dot(tri_b, g2) + _dot(tri_b, g3)
    gc = jnp.where(isf_ref[...] > 0.5, cum_f, cum_b)
    o_ref[0] = jnp.where(isa_ref[...] > 0.5, gc, beta)


def _dn_gates(ab_l, ab_c, a_lane, dt_lane, isa_lane, isf_lane):
    b, l, n = ab_l.shape
    c = DN_CHUNK
    assert ab_c.shape[1] == c and l % c == 0
    n_lt = l // c
    vec = pl.BlockSpec((1, n), lambda bi, i: (0, 0))
    return pl.pallas_call(
        _dn_gate_kernel,
        out_shape=jax.ShapeDtypeStruct((b, l + c, n), F32),
        grid=(b, n_lt + 1),
        in_specs=[pl.BlockSpec((1, c, n), lambda bi, i: (bi, jnp.minimum(i, n_lt - 1), 0)),
                  pl.BlockSpec((1, c, n), lambda bi, i: (bi, 0, 0)), vec, vec, vec, vec],
        out_specs=pl.BlockSpec((1, c, n), lambda bi, i: (bi, i, 0)),
        compiler_params=_cparams(("parallel", "parallel")),
        name="dn_gates",
    )(ab_l, ab_c, a_lane, dt_lane, isa_lane, isf_lane)


def _mm1(a, b):
    return _dot(a.astype(BF16), b.astype(BF16))


def _tri_inverse(ms, in_block, mm):
    c = ms[0].shape[0]
    ri = lax.broadcasted_iota(jnp.int32, (c, c), 0)
    ci = lax.broadcasted_iota(jnp.int32, (c, c), 1)
    eye = jnp.where(ri == ci, 1.0, 0.0).astype(F32)
    mds = [jnp.where(in_block, m, 0.0) for m in ms]
    rests = [m - md for m, md in zip(ms, mds)]
    ts = [eye - md for md in mds]
    ps = [mm(md, md) for md in mds]
    s = 2
    while s < INV_BASE:
        ts = [t + mm(p, t) for p, t in zip(ps, ts)]
        s *= 2
        if s < INV_BASE:
            ps = [mm(p, p) for p in ps]
    ns = [mm(t, rest) for t, rest in zip(ts, rests)]
    rs = [eye - n for n in ns]
    ps = [mm(n, n) for n in ns]
    s = 2
    while s < c // INV_BASE:
        rs = [r + mm(p, r) for p, r in zip(ps, rs)]
        s *= 2
        if s < c // INV_BASE:
            ps = [mm(p, p) for p in ps]
    return [mm(r, t) for r, t in zip(rs, ts)]


def _delta_step(dirs, grow_ref, s_ref):
    c = dirs[0][0].shape[0]
    dh = DN_HEAD
    ri = lax.broadcasted_iota(jnp.int32, (c, c), 0)
    ci = lax.broadcasted_iota(jnp.int32, (c, c), 1)
    sh = jnp.int32(int(math.log2(INV_BASE)))
    in_block = lax.shift_right_logical(ri, sh) == lax.shift_right_logical(ci, sh)
    streams = []
    grams = []
    for d, (q, k, v2, gcol, r0, lane_g0, rev) in enumerate(dirs):
        strict, incl = (ri < ci, ri <= ci) if rev else (ri > ci, ri >= ci)
        lane = lax.broadcasted_iota(jnp.int32, gcol.shape, 1)
        kk = _dot_nt(k, k)
        qk = _dot_nt(q, k)
        grams.append(qk)
        for s in range(2):
            lg = lane_g0 + s
            lb = lg + gcol.shape[1] // 4
            gc_col = jnp.sum(jnp.where(lane == lg, gcol, 0.0), axis=-1, keepdims=True)
            b_col = jnp.sum(jnp.where(lane == lb, gcol, 0.0), axis=-1, keepdims=True)
            gc_row = grow_ref[0, pl.ds(lg, 1), pl.ds(r0, c)]
            decay = jnp.where(incl, jnp.exp(jnp.minimum(gc_col - gc_row, 0.0)), 0.0)
            m = jnp.where(strict, b_col * kk * decay, 0.0)
            streams.append((d, s, gc_col, b_col, decay, m))
    ainvs = _tri_inverse([st[5] for st in streams], in_block, _mm1)

    kqs = []
    for d, (q, k, *_rest) in enumerate(dirs):
        s_cat = jnp.concatenate([s_ref[2 * d], s_ref[2 * d + 1]], axis=1).astype(BF16)
        kqs.append(_dot(jnp.concatenate([k, q], axis=0), s_cat))
    v_news = []
    for (d, s, gc_col, b_col, decay, m), ainv in zip(streams, ainvs):
        v = dirs[d][2][:, s * dh:(s + 1) * dh].astype(F32)
        ks = kqs[d][:c, s * dh:(s + 1) * dh]
        rhs = (b_col * (v - jnp.exp(gc_col) * ks)).astype(BF16)
        v_news.append(_dot(ainv.astype(BF16), rhs).astype(BF16))
    outs = [[None, None] for _ in dirs]
    for (d, s, gc_col, b_col, decay, m), vb in zip(streams, v_news):
        k, rev = dirs[d][1], dirs[d][6]
        qs = kqs[d][c:, s * dh:(s + 1) * dh]
        outs[d][s] = jnp.exp(gc_col) * qs + _dot((grams[d] * decay).astype(BF16), vb)
        last = 0 if rev else c - 1
        gl = gc_col[last:last + 1, :]
        kd = jnp.exp(gl - gc_col) * k.astype(F32)
        s_ref[2 * d + s] = jnp.exp(gl) * s_ref[2 * d + s] + _dot(kd.T.astype(BF16), vb)
    return [jnp.concatenate(o, axis=1) for o in outs]


def _delta_kernel(q_ref, k_ref, v_ref, gcol_ref, grow_ref, o_ref, s_ref, *, n_vh):
    h = pl.program_id(1)
    c = DN_CHUNK
    n_chunks = q_ref.shape[1] // c
    n_lat = n_chunks - 1
    s_ref[...] = jnp.zeros_like(s_ref)
    o_ref[...] = jnp.zeros_like(o_ref)

    def body(i, carry):
        dirs, rows = [], []
        for rev in (False, True):
            lat_ch = (n_lat - i) if rev else (i - 1)
            r0 = pl.multiple_of(jnp.where(i == 0, n_lat, lat_ch) * c, c)
            rows.append(pl.ds(r0, c))
            dirs.append((q_ref[0, rows[-1], :], k_ref[0, rows[-1], :], v_ref[0, rows[-1], :],
                         gcol_ref[0, rows[-1], :], r0, (2 * n_vh if rev else 0) + 2 * h, rev))
        for r, o in zip(rows, _delta_step(dirs, grow_ref, s_ref)):
            o_ref[0, r, :] = o_ref[0, r, :] + o
        return carry

    lax.fori_loop(0, n_chunks, body, 0)


def _delta_rule(qk, v, gcol, grow):
    b, t, qk2 = qk.shape
    dh = DN_HEAD
    n_qh = qk2 // 2 // dh
    n_vh = v.shape[2] // dh
    ng = gcol.shape[2]
    assert t % DN_CHUNK == 0 and n_vh == 2 * n_qh and ng == 4 * n_vh
    kern = functools.partial(_delta_kernel, n_vh=n_vh)
    return pl.pallas_call(
        kern,
        out_shape=jax.ShapeDtypeStruct((b, t, n_vh * dh), F32),
        grid=(b, n_qh),
        in_specs=[
            pl.BlockSpec((1, t, dh), lambda bi, h: (bi, 0, h)),
            pl.BlockSpec((1, t, dh), lambda bi, h: (bi, 0, h + n_qh)),
            pl.BlockSpec((1, t, 2 * dh), lambda bi, h: (bi, 0, h)),
            pl.BlockSpec((1, t, ng), lambda bi, h: (bi, 0, 0)),
            pl.BlockSpec((1, ng, t), lambda bi, h: (bi, 0, 0)),
        ],
        out_specs=pl.BlockSpec((1, t, 2 * dh), lambda bi, h: (bi, 0, h)),
        scratch_shapes=[pltpu.VMEM((4, dh, dh), F32)],
        compiler_params=_cparams(("parallel", "parallel")),
        name="delta_rule",
    )(qk, qk, v, gcol, grow)


def _dn_out_kernel(o_in_ref, z_ref, og_ref, w_ref, x_ref, gt_ref, o_ref, h_scr):
    @pl.when(pl.program_id(2) == 0)
    def _():
        n_h = h_scr.shape[1] // DN_HEAD
        for hh in range(n_h):
            sl = slice(hh * DN_HEAD, (hh + 1) * DN_HEAD)
            o = o_in_ref[0, :, sl]
            ms = jnp.mean(o * o, axis=-1, keepdims=True)
            y = (o * lax.rsqrt(ms + EPS)) * og_ref[...]
            h_scr[:, sl] = (y * _silu(z_ref[0, :, sl].astype(F32))).astype(BF16)

    y = _dot(h_scr[...], w_ref[...])
    o_ref[0] = x_ref[0] + gt_ref[...] * y


def _dn_out_proj(o_all, row_off, proj3, z_off, onorm_g, w_bf, x3, mod3, batch_row, tm):
    b, l, d = x3.shape
    e = o_all.shape[2]
    tn = 512
    assert row_off % tm == 0 and l % tm == 0 and z_off % e == 0 and d % tn == 0
    zoff = z_off // e
    ooff = row_off // tm
    return pl.pallas_call(
        _dn_out_kernel,
        out_shape=jax.ShapeDtypeStruct((b, l, d), F32),
        grid=(b, l // tm, d // tn),
        in_specs=[
            pl.BlockSpec((1, tm, e), lambda bi, i, j: (bi, i + ooff, 0)),
            pl.BlockSpec((1, tm, e), lambda bi, i, j: (bi, i, zoff)),
            pl.BlockSpec((1, DN_HEAD), lambda bi, i, j: (0, 0)),
            pl.BlockSpec((e, tn), lambda bi, i, j: (0, j)),
            pl.BlockSpec((1, tm, tn), lambda bi, i, j: (bi, i, j)),
            pl.BlockSpec((None, 1, tn), lambda bi, i, j: (batch_row(bi), 0, 2 * (d // tn) + j)),
        ],
        out_specs=pl.BlockSpec((1, tm, tn), lambda bi, i, j: (bi, i, j)),
        scratch_shapes=[pltpu.VMEM((tm, e), BF16)],
        compiler_params=_cparams(("parallel", "parallel", "arbitrary")),
        name="dn_out_proj",
    )(o_all, proj3, onorm_g.reshape(1, DN_HEAD), w_bf, x3, mod3)


def _cv_in_kernel(x_ref, sc_ref, sh_ref, g_ref, wa_ref, wg_ref, ba_ref, bg_ref, o_ref, h_scr):
    @pl.when(pl.program_id(1) == 0)
    def _():
        h_scr[...] = _norm_mod(x_ref[...], g_ref[...], sc_ref[...], sh_ref[...]).astype(BF16)

    h = h_scr[...]
    a = _dot(h, wa_ref[...]) + ba_ref[...]
    gate = _dot(h, wg_ref[...]) + bg_ref[...]
    o_ref[...] = (a * jax.nn.sigmoid(gate)).astype(o_ref.dtype)


def _cv_in_proj(x2, mod3, g, w1_bf, b1, row_fn, tm):
    m, d = x2.shape
    ci = w1_bf.shape[1] // 2
    tn = 1024
    nj = ci // tn
    b1r = b1.reshape(1, 2 * ci)
    return pl.pallas_call(
        _cv_in_kernel,
        out_shape=jax.ShapeDtypeStruct((m, ci), BF16),
        grid=(m // tm, nj),
        in_specs=[
            pl.BlockSpec((tm, d), lambda i, j: (i, 0)),
            _mod_spec(d, 1, row_fn),
            _mod_spec(d, 0, row_fn),
            pl.BlockSpec((1, d), lambda i, j: (0, 0)),
            pl.BlockSpec((d, tn), lambda i, j: (0, j)),
            pl.BlockSpec((d, tn), lambda i, j: (0, j + nj)),
            pl.BlockSpec((1, tn), lambda i, j: (0, j)),
            pl.BlockSpec((1, tn), lambda i, j: (0, j + nj)),
        ],
        out_specs=pl.BlockSpec((tm, tn), lambda i, j: (i, j)),
        scratch_shapes=[pltpu.VMEM((tm, d), BF16)],
        compiler_params=_cparams(("parallel", "arbitrary")),
        name="cv_in_proj",
    )(x2, mod3, mod3, g.reshape(1, d), w1_bf, w1_bf, b1r, b1r)


def _cv_out_kernel(u_ref, up_ref, un_ref, dw_ref, dwb_ref, lg_ref, lb_ref, w_ref, b2_ref, x_ref, gt_ref,
                   o_ref, ext_scr, sh_scr, conv_scr, h_scr, *, tm, taps, rb, cch):
    i = pl.program_id(1)
    n_i = pl.num_programs(1)
    hl = CONV_HALO
    sub = 8
    span = sh_scr.shape[1]

    @pl.when(pl.program_id(2) == 0)
    def _():
        prev = up_ref[0].astype(F32)
        nxt = un_ref[0].astype(F32)
        ext_scr[0:hl, :] = jnp.where(i > 0, prev, 0.0)
        ext_scr[hl:hl + tm, :] = u_ref[0].astype(F32)
        ext_scr[hl + tm:, :] = jnp.where(i < n_i - 1, nxt, 0.0)

        for cc in range(ext_scr.shape[1] // cch):
            lanes = slice(cc * cch, (cc + 1) * cch)
            for r in range(1, sub):
                sh_scr[r - 1] = ext_scr[r:r + span, lanes]

            def rows(rbi, carry):
                r0 = pl.multiple_of(rbi * rb, rb)
                acc = None
                for k in range(taps):
                    off = hl - taps // 2 + k
                    r, a = off % sub, off - off % sub
                    if r == 0:
                        src = ext_scr[pl.ds(r0 + a, rb), lanes]
                    else:
                        src = sh_scr[r - 1, pl.ds(r0 + a, rb), :]
                    term = dw_ref[k:k + 1, lanes] * src
                    acc = term if acc is None else acc + term
                conv_scr[pl.ds(r0, rb), lanes] = acc + dwb_ref[:, lanes]
                return carry

            lax.fori_loop(0, tm // rb, rows, 0)

        def norm_rows(rbi, carry):
            r0 = pl.multiple_of(rbi * rb, rb)
            acc = conv_scr[pl.ds(r0, rb), :]
            mu = jnp.mean(acc, axis=-1, keepdims=True)
            xc = acc - mu
            var = jnp.mean(xc * xc, axis=-1, keepdims=True)
            y = (xc * lax.rsqrt(var + EPS)) * lg_ref[...] + lb_ref[...]
            h_scr[pl.ds(r0, rb), :] = _silu(y).astype(BF16)
            return carry

        lax.fori_loop(0, tm // rb, norm_rows, 0)

    y = _dot(h_scr[...], w_ref[...]) + b2_ref[...]
    o_ref[0] = x_ref[0] + gt_ref[...] * y


def _cv_out_proj(u, dw, dwb, ln_g, ln_b, w2_bf, b2, x3, mod3, batch_row, tm):
    b, l, ci = u.shape
    d = w2_bf.shape[1]
    taps = dw.shape[0]
    tn = 1024
    hl = CONV_HALO
    rpb = tm // hl
    last = l // hl - 1
    cch = min(ci, 512)
    kern = functools.partial(_cv_out_kernel, tm=tm, taps=taps, rb=32, cch=cch)
    vec = lambda n: pl.BlockSpec((1, n), lambda bi, i, j: (0, 0))
    return pl.pallas_call(
        kern,
        out_shape=jax.ShapeDtypeStruct((b, l, d), F32),
        grid=(b, l // tm, d // tn),
        in_specs=[
            pl.BlockSpec((1, tm, ci), lambda bi, i, j: (bi, i, 0)),
            pl.BlockSpec((1, hl, ci), lambda bi, i, j: (bi, jnp.maximum(i * rpb - 1, 0), 0)),
            pl.BlockSpec((1, hl, ci), lambda bi, i, j: (bi, jnp.minimum((i + 1) * rpb, last), 0)),
            pl.BlockSpec((taps, ci), lambda bi, i, j: (0, 0)),
            vec(ci), vec(ci), vec(ci),
            pl.BlockSpec((ci, tn), lambda bi, i, j: (0, j)),
            pl.BlockSpec((1, tn), lambda bi, i, j: (0, j)),
            pl.BlockSpec((1, tm, tn), lambda bi, i, j: (bi, i, j)),
            pl.BlockSpec((None, 1, tn), lambda bi, i, j: (batch_row(bi), 0, 2 * (d // tn) + j)),
        ],
        out_specs=pl.BlockSpec((1, tm, tn), lambda bi, i, j: (bi, i, j)),
        scratch_shapes=[pltpu.VMEM((tm + 2 * hl, ci), F32),
                        pltpu.VMEM((7, tm + 2 * hl - 8, cch), F32),
                        pltpu.VMEM((tm, ci), F32),
                        pltpu.VMEM((tm, ci), BF16)],
        compiler_params=_cparams(("parallel", "parallel", "arbitrary")),
        name="cv_out_proj",
    )(u, u, u, dw, dwb.reshape(1, ci), ln_g.reshape(1, ci), ln_b.reshape(1, ci), w2_bf,
      b2.reshape(1, d), x3, mod3)


def _router_kernel(x_ref, sc_ref, sh_ref, g_ref, w_ref, b_ref, h_ref, eid_ref, wt_ref):
    h = _norm_mod(x_ref[...], g_ref[...], sc_ref[...], sh_ref[...])
    h_ref[...] = _pack_bf16_pairs(h)
    logits = _dot3(h, w_ref[...]) + b_ref[...]
    lane = lax.broadcasted_iota(jnp.int32, logits.shape, 1)
    neg = jnp.float32(-1e30)
    big = jnp.int32(ROUTER_LANES)
    is_grp = lane < N_GROUPS
    gl = jnp.where(is_grp, logits, neg)
    gmax = jnp.max(gl, axis=-1, keepdims=True)
    gsum = jnp.sum(jnp.where(is_grp, jnp.exp(gl - gmax), 0.0), axis=-1, keepdims=True)
    grp_p = 1.0 / gsum
    gidx = jnp.min(jnp.where(gl == gmax, lane, big), axis=-1, keepdims=True)
    lo = N_GROUPS + gidx * EXPERTS_PER_GROUP
    in_g = jnp.logical_and(lane >= lo, lane < lo + EXPERTS_PER_GROUP)
    el = jnp.where(in_g, logits, neg)
    emax = jnp.max(el, axis=-1, keepdims=True)
    ex = jnp.where(in_g, jnp.exp(el - emax), 0.0)
    prob = jnp.where(in_g, ex / jnp.sum(ex, axis=-1, keepdims=True), -1.0)
    p1 = jnp.max(prob, axis=-1, keepdims=True)
    i1 = jnp.min(jnp.where(prob == p1, lane, big), axis=-1, keepdims=True)
    prob2 = jnp.where(lane == i1, -1.0, prob)
    p2 = jnp.max(prob2, axis=-1, keepdims=True)
    i2 = jnp.min(jnp.where(prob2 == p2, lane, big), axis=-1, keepdims=True)
    denom = p1 + p2
    w1 = grp_p * p1 / denom
    w2 = grp_p * p2 / denom
    eid_ref[...] = jnp.where(lane == 0, i1 - N_GROUPS, jnp.where(lane == 1, i2 - N_GROUPS, 0))
    wt_ref[...] = jnp.where(lane == 0, w1, jnp.where(lane == 1, w2, 0.0))


def _router(x2, mod3, g, w_r, b_r, row_fn, tm):
    m, d = x2.shape
    n = ROUTER_LANES
    return pl.pallas_call(
        _router_kernel,
        out_shape=(jax.ShapeDtypeStruct((m, d // 2), jnp.uint32), jax.ShapeDtypeStruct((m, n), jnp.int32),
                   jax.ShapeDtypeStruct((m, n), F32)),
        grid=(m // tm, 1),
        in_specs=[
            pl.BlockSpec((tm, d), lambda i, j: (i, 0)),
            _mod_spec(d, 4, row_fn),
            _mod_spec(d, 3, row_fn),
            pl.BlockSpec((1, d), lambda i, j: (0, 0)),
            pl.BlockSpec((d, n), lambda i, j: (0, 0)),
            pl.BlockSpec((1, n), lambda i, j: (0, 0)),
        ],
        out_specs=(pl.BlockSpec((tm, d // 2), lambda i, j: (i, 0)),
                   pl.BlockSpec((tm, n), lambda i, j: (i, 0)),
                   pl.BlockSpec((tm, n), lambda i, j: (i, 0))),
        compiler_params=_cparams(("parallel", "arbitrary")),
        name="moe_router",
    )(x2, mod3, mod3, g.reshape(1, d), w_r, b_r)


def _row_gather_start(idx_ref, n_pairs, src_hbm, dst, sem):
    def issue(j, carry):
        for u in range(2):
            r = 2 * j + u
            pltpu.make_async_copy(src_hbm.at[pl.ds(idx_ref[0, 0, r], 1), :], dst.at[pl.ds(r, 1), :],
                                  sem).start(priority=u)
        return carry

    lax.fori_loop(0, n_pairs, issue, 0)


def _row_gather_wait(n_pairs, dst, sem):
    def wait_pair(j, carry):
        pltpu.make_async_copy(dst.at[pl.ds(0, 2)], dst.at[pl.ds(0, 2)], sem).wait()
        return carry

    lax.fori_loop(0, n_pairs, wait_pair, 0)


def _prefetched_gather(idx_ref, idx_next_ref, pairs, pairs_next, src_hbm, buf, sem):
    i = pl.program_id(0)
    slot = lax.rem(i, 2)

    @pl.when(i == 0)
    def _():
        _row_gather_start(idx_ref, pairs, src_hbm, buf.at[0], sem.at[0])

    @pl.when(pairs_next > 0)
    def _():
        _row_gather_start(idx_next_ref, pairs_next, src_hbm, buf.at[1 - slot], sem.at[1 - slot])

    @pl.when(pairs > 0)
    def _():
        _row_gather_wait(pairs, buf.at[slot], sem.at[slot])

    return slot


def _expert_kernel(be_ref, np_ref, idx_ref, idx_next_ref, h_hbm, wgu_ref, wd_ref, o_ref,
                   xbuf, sem, wgu_bf, wd_bf):
    i = pl.program_id(0)
    n = pl.num_programs(0)

    @pl.when(i == 0)
    def _():
        xbuf[...] = jnp.zeros_like(xbuf)

    pairs = np_ref[i]
    pairs_next = jnp.where(i + 1 < n, np_ref[jnp.minimum(i + 1, n - 1)], 0)
    slot = _prefetched_gather(idx_ref, idx_next_ref, pairs, pairs_next, h_hbm, xbuf, sem)
    prev = be_ref[jnp.maximum(i - 1, 0)]
    changed = jnp.logical_or(i == 0, be_ref[i] != prev)

    @pl.when(changed)
    def _():
        wgu_bf[...] = wgu_ref[...].astype(BF16)
        wd_bf[...] = wd_ref[...].astype(BF16)

    @pl.when(pairs > 0)
    def _():
        de = wd_bf.shape[0]
        x = _unpack_bf16_pairs(xbuf[slot]).astype(BF16)
        gu = _dot(x, wgu_bf[...])
        hmid = (_silu(gu[:, :de]) * gu[:, de:]).astype(BF16)
        o_ref[...] = _pack_bf16_pairs(_dot(hmid, wd_bf[...]))

    @pl.when(pairs == 0)
    def _():
        o_ref[...] = jnp.zeros_like(o_ref)


def _experts(block_expert, n_pairs, slot_tok, h_packed, w_gu, w_down, layer):
    n_blocks = slot_tok.shape[0]
    dp = h_packed.shape[1]
    d = 2 * dp
    de2 = w_gu.shape[3]
    de = de2 // 2
    grid_spec = pltpu.PrefetchScalarGridSpec(
        num_scalar_prefetch=2,
        grid=(n_blocks,),
        in_specs=[
            pl.BlockSpec((1, 1, MOE_BLOCK), lambda i, be, nu: (i, 0, 0), memory_space=pltpu.SMEM),
            pl.BlockSpec((1, 1, MOE_BLOCK), lambda i, be, nu: (jnp.minimum(i + 1, n_blocks - 1), 0, 0),
                         memory_space=pltpu.SMEM),
            pl.BlockSpec(memory_space=pl.ANY),
            pl.BlockSpec((None, None, d, de2), lambda i, be, nu: (layer, be[i], 0, 0)),
            pl.BlockSpec((None, None, de, d), lambda i, be, nu: (layer, be[i], 0, 0)),
        ],
        out_specs=pl.BlockSpec((MOE_BLOCK, dp), lambda i, be, nu: (i, 0)),
        scratch_shapes=[pltpu.VMEM((2, MOE_BLOCK, dp), jnp.uint32), pltpu.SemaphoreType.DMA((2,)),
                        pltpu.VMEM((d, de2), BF16), pltpu.VMEM((de, d), BF16)],
    )
    return pl.pallas_call(
        _expert_kernel,
        out_shape=jax.ShapeDtypeStruct((n_blocks * MOE_BLOCK, dp), jnp.uint32),
        grid_spec=grid_spec,
        compiler_params=_cparams(("arbitrary",)),
        name="moe_experts",
    )(block_expert, n_pairs, slot_tok, slot_tok, h_packed, w_gu, w_down)


def _combine_kernel(idx_ref, idx_next_ref, ys_hbm, x_ref, w_ref, gt_ref, o_ref, ybuf, sem, *, tm):
    pairs_next = jnp.where(pl.program_id(0) + 1 < pl.num_programs(0), tm, 0)
    slot = _prefetched_gather(idx_ref, idx_next_ref, jnp.int32(tm), pairs_next, ys_hbm, ybuf, sem)
    w = w_ref[...]
    y0 = _unpack_bf16_pairs(ybuf[slot, 0:tm, :])
    y1 = _unpack_bf16_pairs(ybuf[slot, tm:2 * tm, :])
    o_ref[...] = x_ref[...] + gt_ref[...] * (w[:, 0:1] * y0 + w[:, 1:2] * y1)


def _combine(x2, ys_packed, slot_of_assign, wts, mod3, row_fn, tm):
    m, d = x2.shape
    dp = ys_packed.shape[1]
    assert m % tm == 0 and d == 2 * dp
    n_t = m // tm
    idx = slot_of_assign.reshape(n_t, tm, 2).transpose(0, 2, 1).reshape(n_t, 1, 2 * tm)
    kern = functools.partial(_combine_kernel, tm=tm)
    return pl.pallas_call(
        kern,
        out_shape=jax.ShapeDtypeStruct((m, d), F32),
        grid=(n_t, 1),
        in_specs=[
            pl.BlockSpec((1, 1, 2 * tm), lambda i, j: (i, 0, 0), memory_space=pltpu.SMEM),
            pl.BlockSpec((1, 1, 2 * tm), lambda i, j: (jnp.minimum(i + 1, n_t - 1), 0, 0),
                         memory_space=pltpu.SMEM),
            pl.BlockSpec(memory_space=pl.ANY),
            pl.BlockSpec((tm, d), lambda i, j: (i, 0)),
            pl.BlockSpec((tm, ROUTER_LANES), lambda i, j: (i, 0)),
            _mod_spec(d, 5, row_fn),
        ],
        out_specs=pl.BlockSpec((tm, d), lambda i, j: (i, 0)),
        scratch_shapes=[pltpu.VMEM((2, 2 * tm, dp), jnp.uint32), pltpu.SemaphoreType.DMA((2,))],
        compiler_params=_cparams(("arbitrary", "arbitrary")),
        name="moe_combine",
    )(idx, idx, ys_packed, x2, wts, mod3)


def _final_kernel(x_ref, g_ref, o_ref):
    x = x_ref[...]
    ms = jnp.mean(x * x, axis=-1, keepdims=True)
    o_ref[...] = (x * lax.rsqrt(ms + EPS)) * g_ref[...]


def _final_norm(x2, g, tm):
    m, d = x2.shape
    return pl.pallas_call(
        _final_kernel,
        out_shape=jax.ShapeDtypeStruct((m, d), F32),
        grid=(m // tm,),
        in_specs=[pl.BlockSpec((tm, d), lambda i: (i, 0)), pl.BlockSpec((1, d), lambda i: (0, 0))],
        out_specs=pl.BlockSpec((tm, d), lambda i: (i, 0)),
        compiler_params=_cparams(("parallel",)),
        name="final_norm",
    )(x2, g.reshape(1, d))


def _to_column_major(h):
    b, n, d = h.shape
    rows = n // GRID_W
    return h.reshape(b, rows, GRID_W, d).transpose(0, 2, 1, 3).reshape(b, n, d)


def _from_column_major(h):
    b, n, d = h.shape
    rows = n // GRID_W
    return h.reshape(b, GRID_W, rows, d).transpose(0, 2, 1, 3).reshape(b, n, d)


def _dispatch_tables(eid):
    n_t, top_k = eid.shape
    n_assign = n_t * top_k
    i32 = jnp.int32
    e_flat = eid.reshape(-1)
    order = jnp.argsort(e_flat).astype(i32)
    rank = jnp.argsort(order).astype(i32)
    onehot = e_flat[:, None] == jnp.arange(N_EXPERTS, dtype=i32)[None, :]
    counts = jnp.sum(onehot.astype(i32), axis=0)
    padded = (counts + MOE_BLOCK - 1) // MOE_BLOCK * MOE_BLOCK
    starts = jnp.cumsum(counts) - counts
    pad_ends = jnp.cumsum(padded)
    pad_starts = pad_ends - padded
    lookup = lambda table: jnp.sum(jnp.where(onehot, table[None, :], 0), axis=1)
    slot_of_assign = lookup(pad_starts - starts) + rank
    n_blocks = -(-n_assign // MOE_BLOCK) + N_EXPERTS
    n_used = (pad_ends[-1] // MOE_BLOCK).astype(i32)
    blk = jnp.arange(n_blocks, dtype=i32)
    blk_c = jnp.minimum(blk, n_used - 1)
    block_expert = jnp.minimum(
        jnp.sum((pad_ends[None, :] <= (blk_c * MOE_BLOCK)[:, None]).astype(i32), axis=1), N_EXPERTS - 1)
    be_hot = block_expert[:, None] == jnp.arange(N_EXPERTS, dtype=i32)[None, :]
    blk_lookup = lambda table: jnp.sum(jnp.where(be_hot, table[None, :], 0), axis=1)
    row0 = blk_c * MOE_BLOCK - blk_lookup(pad_starts)
    n_valid = jnp.where(blk < n_used, blk_lookup(counts) - row0, 0)
    r = jnp.arange(MOE_BLOCK, dtype=i32)[None, :]
    pos = (blk_lookup(starts) + row0)[:, None] + r
    valid = r < n_valid[:, None]
    src = jnp.take(order, jnp.clip(pos, 0, n_assign - 1).reshape(-1), mode="clip")
    slot_tok = jnp.where(valid, src.reshape(n_blocks, MOE_BLOCK) // top_k, 0)
    n_pairs = (jnp.minimum(n_valid, MOE_BLOCK) + 1) // 2
    return (slot_tok.reshape(n_blocks, 1, MOE_BLOCK), slot_of_assign.reshape(n_t, top_k), block_expert,
            n_pairs.astype(i32))


def _moe(h_packed, eid, w_gu, w_down, layer):
    slot_tok, slot_of_assign, block_expert, n_pairs = _dispatch_tables(eid)
    ys = _experts(block_expert, n_pairs, slot_tok, h_packed, w_gu, w_down, layer)
    return ys, slot_of_assign


def _gate_lane_params(a_log, dt_bias):
    n_dir, n_h = a_log.shape
    a = jnp.exp(a_log.astype(F32))
    zeros = jnp.zeros_like(a)
    ones = jnp.ones_like(a)
    lane = lambda av, bv: jnp.stack([av, bv], axis=1).reshape(1, n_dir * 2 * n_h)
    isf = jnp.broadcast_to((jnp.arange(n_dir) == 0).astype(F32)[:, None], a.shape)
    return lane(a, zeros), lane(dt_bias.astype(F32), zeros), lane(ones, zeros), lane(isf, isf)


def kernel(x, c, ctx, c_ctx, ada_w, ada_b, norm1_g, norm2_g, dn_w_in, dn_conv_w, dn_a_log, dn_dt_bias, dn_onorm_g, dn_w_out, cv_w1, cv_b1, cv_dw, cv_dwb, cv_ln_g, cv_ln_b, cv_w2, cv_b2, moe_w_grp, moe_b_grp, moe_w_exp, moe_b_exp, moe_w_gu, moe_w_down, final_g):
    bsz, n_lat, d = x.shape
    lc = ctx.shape[1]
    depth = ada_w.shape[0]
    conv_dim = dn_conv_w.shape[2]
    v_dim = dn_w_out.shape[1]
    qk_dim = (conv_dim - v_dim) // 2
    ml, mc = bsz * n_lat, bsz * lc
    tm_l = min(n_lat, 1024)
    tm_c = min(mc, 1024)
    tm_s = min(n_lat, mc, 512)
    ctx_row = bsz
    lat_row = lambda i: (i * tm_l) // n_lat
    lat_row_s = lambda i: (i * tm_s) // n_lat
    tm_g = min(n_lat, mc, 256)
    lat_row_g = lambda i: (i * tm_g) // n_lat
    ctx_rowf = lambda i: ctx_row

    cvec = jnp.concatenate([c, c_ctx[None, :], jnp.zeros((8 - bsz - 1, d), F32)], axis=0)
    mods = _ada_mod(cvec, ada_w, ada_b)

    xl = x.reshape(ml, d)
    xc = ctx.reshape(mc, d)
    col_major_now = False
    for i in range(depth):
        last = i == depth - 1
        j = i // 2
        mod3 = mods[i].reshape(8, 1, 6 * d)
        col_major = (i // 2) % 2 == 1
        if col_major != col_major_now:
            xl3 = xl.reshape(bsz, n_lat, d)
            xl3 = _to_column_major(xl3) if col_major else _from_column_major(xl3)
            xl = xl3.reshape(ml, d)
            col_major_now = col_major

        if i % 2 == 0:
            w_in = dn_w_in[j]
            w_main = w_in[:, :conv_dim + v_dim].astype(BF16)
            w_ab = w_in[:, conv_dim + v_dim:]
            lanes = _gate_lane_params(dn_a_log[j], dn_dt_bias[j])
            n_ab = w_ab.shape[1]
            n_proj = conv_dim + v_dim
            proj_c, ab_c = _dn_in_proj(xc, mod3, norm1_g[i], w_main, w_ab, ctx_rowf, tm_c)
            proj_l, ab_l = _dn_in_proj(xl, mod3, norm1_g[i], w_main, w_ab, lat_row, tm_l)
            proj_c = proj_c.reshape(bsz, lc, n_proj)
            proj_l = proj_l.reshape(bsz, n_lat, n_proj)
            qk = _dn_conv(proj_l, proj_c, dn_conv_w[j], 0, 2 * qk_dim, True, qk_dim)
            v = _dn_conv(proj_l, proj_c, dn_conv_w[j], 2 * qk_dim, v_dim, False, qk_dim)
            gcol = _dn_gates(ab_l.reshape(bsz, n_lat, n_ab), ab_c.reshape(bsz, lc, n_ab), *lanes)
            o_all = _delta_rule(qk, v, gcol, jnp.swapaxes(gcol, 1, 2))
            w_out = dn_w_out[j].astype(BF16)
            xl = _dn_out_proj(o_all, 0, proj_l, conv_dim, dn_onorm_g[j], w_out, xl.reshape(bsz, n_lat, d),
                              mod3, lambda bi: bi, tm_s).reshape(ml, d)
            xc = _dn_out_proj(o_all, n_lat, proj_c, conv_dim, dn_onorm_g[j], w_out, xc.reshape(bsz, lc, d),
                              mod3, lambda bi: ctx_row, lc).reshape(mc, d)
        else:
            w1 = cv_w1[j].astype(BF16)
            w2 = cv_w2[j].astype(BF16)
            ci = w2.shape[0]
            cv_tail = (cv_dw[j], cv_dwb[j], cv_ln_g[j], cv_ln_b[j], w2, cv_b2[j])
            ul = _cv_in_proj(xl, mod3, norm1_g[i], w1, cv_b1[j], lat_row, tm_l)
            xl = _cv_out_proj(ul.reshape(bsz, n_lat, ci), *cv_tail, xl.reshape(bsz, n_lat, d), mod3,
                              lambda bi: bi, min(n_lat, 512)).reshape(ml, d)
            if not last:
                uc = _cv_in_proj(xc, mod3, norm1_g[i], w1, cv_b1[j], ctx_rowf, tm_c)
                xc = _cv_out_proj(uc.reshape(bsz, lc, ci), *cv_tail, xc.reshape(bsz, lc, d), mod3,
                                  lambda bi: ctx_row, lc).reshape(mc, d)

        w_r = jnp.concatenate([moe_w_grp[i], moe_w_exp[i],
                               jnp.zeros((d, ROUTER_LANES - N_GROUPS - N_EXPERTS), F32)], axis=1)
        b_r = jnp.concatenate([moe_b_grp[i], moe_b_exp[i],
                               jnp.zeros((ROUTER_LANES - N_GROUPS - N_EXPERTS,), F32)])[None, :]
        hl2, eid_l, wt_l = _router(xl, mod3, norm2_g[i], w_r, b_r, lat_row, tm_l)
        if last:
            ys, slots = _moe(hl2, eid_l[:, :2], moe_w_gu, moe_w_down, i)
            xl = _combine(xl, ys, slots, wt_l, mod3, lat_row_g, tm_g)
        else:
            hc2, eid_c, wt_c = _router(xc, mod3, norm2_g[i], w_r, b_r, ctx_rowf, tm_c)
            h_all = jnp.concatenate([hc2, hl2], axis=0)
            eid_all = jnp.concatenate([eid_c[:, :2], eid_l[:, :2]], axis=0)
            ys, slots = _moe(h_all, eid_all, moe_w_gu, moe_w_down, i)
            xc = _combine(xc, ys, slots[:mc], wt_c, mod3, ctx_rowf, tm_g)
            xl = _combine(xl, ys, slots[mc:], wt_l, mod3, lat_row_g, tm_g)

    out = _final_norm(xl, final_g, tm_l).reshape(bsz, n_lat, d)
    if col_major_now:
        out = _from_column_major(out)
    return out
```
